```python
import jax, jax.numpy as jnp
from jax import lax
import numpy as np

D_MODEL = 1024
BATCH = 8
SEQ = 4096
DEPTH = 2

HEAD_DIM = 64
D_A = D_MODEL // 4
D_B = 3 * D_MODEL // 8
D_C = 3 * D_MODEL // 8
D_MIX = D_A + D_B + D_C
N_HEADS_A = D_A // HEAD_DIM
N_BLOCKS_B = D_B // HEAD_DIM
N_HEADS_C = D_C // HEAD_DIM
CHUNK = 128
CONV_B = 4
LRU_C = 8.0
LORA_W = 64
LORA_A = 64
LORA_G = 128
D_FF = 2816
CONV_FF = 3
N_MOD = 6
EPS = 1e-6
LN_EPS = 1e-5
GN_EPS = 64e-5
P_A = 2 * D_A
P_B = 2 * D_B
P_C = 3 * D_C + LORA_W + LORA_A + LORA_G
P_IN = P_A + P_B + P_C

kernel_name = "hybrid_sgu_rglru_rwkv7_adaln"


def rms_norm(x, g):
    xf = x.astype(jnp.float32)
    y = xf * lax.rsqrt(jnp.mean(xf * xf, axis=-1, keepdims=True) + EPS)
    return (y * g.astype(jnp.float32)).astype(x.dtype)


def causal_dwconv(x, w, b):
    k_w = w.shape[0]
    s = x.shape[1]
    xp = jnp.pad(x, ((0, 0), (k_w - 1, 0), (0, 0)))
    y = b + w[0] * xp[:, 0:s]
    for j in range(1, k_w):
        y = y + w[j] * xp[:, j:j + s]
    return y


def token_shift(x):
    return jnp.pad(x[:, :-1], ((0, 0), (1, 0), (0, 0)))


def chunked_sgu(p, ln_g, ln_b, w_s, b_s):
    z = jax.nn.gelu(p)
    u, v = jnp.split(z, 2, axis=-1)
    vf = v.astype(jnp.float32)
    mu = jnp.mean(vf, axis=-1, keepdims=True)
    var = jnp.mean(jnp.square(vf - mu), axis=-1, keepdims=True)
    v = ((vf - mu) * lax.rsqrt(var + LN_EPS) * ln_g.astype(jnp.float32) + ln_b.astype(jnp.float32)).astype(p.dtype)
    bn, s, _ = v.shape
    v = v.reshape(bn, s // CHUNK, CHUNK, N_HEADS_A, HEAD_DIM)
    mask = jnp.tril(jnp.ones((CHUNK, CHUNK), dtype=bool))
    w = jnp.where(mask, w_s, 0.0)
    mixed = jnp.einsum('hts,bnshd->bnthd', w, v) + b_s.T[None, None, :, :, None]
    return u * mixed.reshape(bn, s, D_A)


def rg_lru_block(p, conv_w, conv_b, w_ra, b_ra, w_ix, b_ix, lam):
    xr, yg = jnp.split(p, 2, axis=-1)
    xr = causal_dwconv(xr, conv_w, conv_b)
    bn, s, _ = xr.shape
    xh = xr.reshape(bn, s, N_BLOCKS_B, HEAD_DIM)
    r = jax.nn.sigmoid(jnp.einsum('bshi,hij->bshj', xh, w_ra).reshape(bn, s, D_B) + b_ra)
    i = jax.nn.sigmoid(jnp.einsum('bshi,hij->bshj', xh, w_ix).reshape(bn, s, D_B) + b_ix)
    log_a = (-LRU_C * r.astype(jnp.float32)) * jax.nn.softplus(-lam.astype(jnp.float32))
    a = jnp.exp(log_a)
    bterm = jnp.sqrt(-jnp.expm1(2.0 * log_a)) * (i * xr).astype(jnp.float32)

    def combine(left, right):
        a1, b1 = left
        a2, b2 = right
        return a1 * a2, a2 * b1 + b2

    _, h = lax.associative_scan(combine, (a, bterm), axis=1)
    return jax.nn.gelu(yg) * h.astype(p.dtype)


def rwkv7_time_mix(p, mu, w0, w2, a0, a2, g2, k_k, k_a, r_k, ln_w, ln_b):
    p = p + (token_shift(p) - p) * mu
    r, k, v, xw, xa, xg = jnp.split(
        p, [D_C, 2 * D_C, 3 * D_C, 3 * D_C + LORA_W, 3 * D_C + LORA_W + LORA_A], axis=-1)
    w = -jax.nn.softplus(-(w0 + jnp.tanh(xw) @ w2)) - 0.5
    decay = jnp.exp(-jnp.exp(w.astype(jnp.float32)))
    a = jax.nn.sigmoid(a0 + xa @ a2)
    g = jax.nn.sigmoid(xg) @ g2
    bn, s, _ = r.shape

    def heads(t):
        return t.astype(jnp.float32).reshape(bn, s, N_HEADS_C, HEAD_DIM)

    kk = heads(k * k_k)
    kk = kk * lax.rsqrt(jnp.maximum(jnp.sum(kk * kk, axis=-1, keepdims=True), 1e-24))
    k = k * (1.0 + (a - 1.0) * k_a)
    rh, kh, vh, wh, ah = heads(r), heads(k), heads(v), heads(decay), heads(a)

    def step(state, inp):
        r_t, w_t, k_t, v_t, kk_t, a_t = inp
        sa = jnp.einsum('bhvk,bhk->bhv', state, kk_t)
        state = (state * w_t[:, :, None, :]
                 - sa[..., :, None] * (kk_t * a_t)[..., None, :]
                 + v_t[..., :, None] * k_t[..., None, :])
        return state, jnp.einsum('bhvk,bhk->bhv', state, r_t)

    seq_first = [jnp.moveaxis(t, 1, 0) for t in (rh, wh, kh, vh, kk, ah)]
    state0 = jnp.zeros((bn, N_HEADS_C, HEAD_DIM, HEAD_DIM), jnp.float32)
    _, out = lax.scan(step, state0, tuple(seq_first))
    out = jnp.moveaxis(out, 0, 1)
    mean = jnp.mean(out, axis=-1, keepdims=True)
    var = jnp.mean(jnp.square(out - mean), axis=-1, keepdims=True)
    out = ((out - mean) * lax.rsqrt(var + GN_EPS) * ln_w.astype(jnp.float32).reshape(N_HEADS_C, HEAD_DIM)
           + ln_b.astype(jnp.float32).reshape(N_HEADS_C, HEAD_DIM))
    bonus = jnp.sum(rh * kh * r_k.astype(jnp.float32), axis=-1, keepdims=True) * vh
    out = (out + bonus).reshape(bn, s, D_C).astype(p.dtype)
    return out * g


def conv_glu_ffn(h, w_up, conv_w, conv_b, w_down):
    gate, val = jnp.split(h @ w_up, 2, axis=-1)
    gate = causal_dwconv(gate, conv_w, conv_b)
    return (jax.nn.silu(gate) * val) @ w_down


def setup_inputs(seed: int = 0) -> dict:
    key = jax.random.key(seed)
    ks = jax.random.split(key, 40)
    f32 = jnp.float32
    nrm = lambda k, shape, s: jax.random.normal(k, shape, f32) * s
    L = DEPTH
    u_a = jax.random.uniform(ks[15], (L, D_B), f32, 0.9, 0.999)
    s_a = u_a ** (1.0 / LRU_C)
    return {
        "x": nrm(ks[0], (BATCH, SEQ, D_MODEL), 1.0),
        "c": nrm(ks[1], (BATCH, D_MODEL), 1.0),
        "w_mod": nrm(ks[2], (L, D_MODEL, N_MOD * D_MODEL), 0.5 * D_MODEL ** -0.5),
        "b_mod": nrm(ks[3], (L, N_MOD * D_MODEL), 0.02),
        "norm_mix": 1.0 + nrm(ks[4], (L, D_MODEL), 0.02),
        "w_in": nrm(ks[5], (L, D_MODEL, P_IN), D_MODEL ** -0.5),
        "w_out": nrm(ks[6], (L, D_MIX, D_MODEL), D_MIX ** -0.5),
        "sgu_ln_g": 1.0 + nrm(ks[7], (L, D_A), 0.02),
        "sgu_ln_b": nrm(ks[8], (L, D_A), 0.02),
        "sgu_w": nrm(ks[9], (L, N_HEADS_A, CHUNK, CHUNK), 0.05),
        "sgu_b": 1.0 + nrm(ks[10], (L, N_HEADS_A, CHUNK), 0.02),
        "lru_conv_w": nrm(ks[11], (L, CONV_B, D_B), CONV_B ** -0.5),
        "lru_conv_b": nrm(ks[12], (L, D_B), 0.02),
        "lru_w_a": nrm(ks[13], (L, N_BLOCKS_B, HEAD_DIM, HEAD_DIM), HEAD_DIM ** -0.5),
        "lru_b_a": nrm(ks[14], (L, D_B), 0.02),
        "lru_w_x": nrm(ks[16], (L, N_BLOCKS_B, HEAD_DIM, HEAD_DIM), HEAD_DIM ** -0.5),
        "lru_b_x": nrm(ks[17], (L, D_B), 0.02),
        "lru_lambda": jnp.log(s_a) - jnp.log1p(-s_a),
        "rwkv_mu": jax.random.uniform(ks[18], (L, P_C), f32, 0.0, 1.0),
        "rwkv_w0": jax.random.uniform(ks[19], (L, D_C), f32, -6.0, -1.0),
        "rwkv_w2": nrm(ks[20], (L, LORA_W, D_C), 0.5 * LORA_W ** -0.5),
        "rwkv_a0": nrm(ks[21], (L, D_C), 0.1),
        "rwkv_a2": nrm(ks[22], (L, LORA_A, D_C), 0.5 * LORA_A ** -0.5),
        "rwkv_g2": nrm(ks[23], (L, LORA_G, D_C), LORA_G ** -0.5),
        "rwkv_k_k": 0.85 + nrm(ks[24], (L, D_C), 0.02),
        "rwkv_k_a": 1.0 + nrm(ks[25], (L, D_C), 0.02),
        "rwkv_r_k": nrm(ks[26], (L, N_HEADS_C, HEAD_DIM), 0.1),
        "rwkv_ln_w": 1.0 + nrm(ks[27], (L, D_C), 0.02),
        "rwkv_ln_b": nrm(ks[28], (L, D_C), 0.02),
        "norm_ffn": 1.0 + nrm(ks[29], (L, D_MODEL), 0.02),
        "ffn_w_up": nrm(ks[30], (L, D_MODEL, 2 * D_FF), D_MODEL ** -0.5),
        "ffn_conv_w": nrm(ks[31], (L, CONV_FF, D_FF), CONV_FF ** -0.5),
        "ffn_conv_b": nrm(ks[32], (L, D_FF), 0.02),
        "ffn_w_down": nrm(ks[33], (L, D_FF, D_MODEL), D_FF ** -0.5),
        "norm_final": 1.0 + nrm(ks[34], (D_MODEL,), 0.02),
    }


def reference(x, c, w_mod, b_mod, norm_mix, w_in, w_out,
              sgu_ln_g, sgu_ln_b, sgu_w, sgu_b,
              lru_conv_w, lru_conv_b, lru_w_a, lru_b_a, lru_w_x, lru_b_x, lru_lambda,
              rwkv_mu, rwkv_w0, rwkv_w2, rwkv_a0, rwkv_a2, rwkv_g2, rwkv_k_k, rwkv_k_a,
              rwkv_r_k, rwkv_ln_w, rwkv_ln_b,
              norm_ffn, ffn_w_up, ffn_conv_w, ffn_conv_b, ffn_w_down, norm_final):
    c_act = jax.nn.silu(c)
    for l in range(DEPTH):
        mod = c_act @ w_mod[l] + b_mod[l]
        sh1, sc1, gt1, sh2, sc2, gt2 = [m[:, None, :] for m in jnp.split(mod, N_MOD, axis=-1)]
        h = rms_norm(x, norm_mix[l]) * (1.0 + sc1) + sh1
        p = h @ w_in[l]
        p_a, p_b, p_c = jnp.split(p, [P_A, P_A + P_B], axis=-1)
        y_a = chunked_sgu(p_a, sgu_ln_g[l], sgu_ln_b[l], sgu_w[l], sgu_b[l])
        y_b = rg_lru_block(p_b, lru_conv_w[l], lru_conv_b[l], lru_w_a[l], lru_b_a[l],
                           lru_w_x[l], lru_b_x[l], lru_lambda[l])
        y_c = rwkv7_time_mix(p_c, rwkv_mu[l], rwkv_w0[l], rwkv_w2[l], rwkv_a0[l], rwkv_a2[l],
                             rwkv_g2[l], rwkv_k_k[l], rwkv_k_a[l], rwkv_r_k[l],
                             rwkv_ln_w[l], rwkv_ln_b[l])
        y = jnp.concatenate([y_a, y_b, y_c], axis=-1) @ w_out[l]
        x = x + gt1 * y
        h = rms_norm(x, norm_ffn[l]) * (1.0 + sc2) + sh2
        x = x + gt2 * conv_glu_ffn(h, ffn_w_up[l], ffn_conv_w[l], ffn_conv_b[l], ffn_w_down[l])
    return rms_norm(x, norm_final)
```

```python
import functools

import jax
import jax.numpy as jnp
from jax import lax
from jax.experimental import pallas as pl
from jax.experimental.pallas import tpu as pltpu

HEAD_DIM = 64
SGU_CHUNK = 128
CONV_B = 4
LRU_C = 8.0
LORA_W = 64
LORA_A = 64
LORA_G = 128
CONV_FF = 3
N_MOD = 6
EPS = 1e-6
LN_EPS = 1e-5
GN_EPS = 64e-5

RWKV_CHUNK = 64
LANES = 128
SUBLANES = 8
VMEM_LIMIT = 56 * 1024 * 1024

BF16 = jnp.bfloat16
F32 = jnp.float32


def _dot(a, b):
    return jnp.dot(a, b, preferred_element_type=F32)


def _dot_nt(a, b):
    return lax.dot_general(a, b, (((1,), (1,)), ((), ())), preferred_element_type=F32)


def _dot_tn(a, b):
    return lax.dot_general(a, b, (((0,), (0,)), ((), ())), preferred_element_type=F32)


def _split(x):
    hi = x.astype(BF16)
    lo = (x - hi.astype(F32)).astype(BF16)
    return hi, lo


def _dot_split(x, w_bf16):
    hi, lo = _split(x)
    return _dot(hi, w_bf16) + _dot(lo, w_bf16)


def _softplus(z):
    return jnp.maximum(z, 0.0) + jnp.log1p(jnp.exp(-jnp.abs(z)))


def _params(sem):
    return pltpu.CompilerParams(dimension_semantics=sem, vmem_limit_bytes=VMEM_LIMIT)


def _mod_kernel(c_ref, w_ref, b_ref, o_ref):
    c = c_ref[...]
    ca = c * jax.nn.sigmoid(c)
    w = w_ref[0]
    chi, clo = _split(ca)
    whi, wlo = _split(w)
    acc = _dot(chi, whi) + _dot(chi, wlo) + _dot(clo, whi)
    o_ref[0] = acc + b_ref[0]


def _modulation(c, w_mod, b_mod):
    depth, d, nd = w_mod.shape
    bsz = c.shape[0]
    nblk = nd // d
    return pl.pallas_call(
        _mod_kernel,
        grid=(depth, nblk),
        in_specs=[
            pl.BlockSpec((bsz, d), lambda l, j: (0, 0)),
            pl.BlockSpec((1, d, d), lambda l, j: (l, 0, j)),
            pl.BlockSpec((1, 1, d), lambda l, j: (l, 0, j)),
        ],
        out_specs=pl.BlockSpec((1, bsz, d), lambda l, j: (l, 0, j)),
        out_shape=jax.ShapeDtypeStruct((depth, bsz, nd), F32),
        compiler_params=_params(("arbitrary", "arbitrary")),
        name="adaln_mod",
    )(c, w_mod, b_mod.reshape(depth, 1, nd))


def _in_proj_kernel(x_ref, mod_ref, g_ref, wa_ref, wb_ref, wc_ref, pa_ref, pb_ref, pc_ref, *, d):
    x = x_ref[0]
    y = x * lax.rsqrt(jnp.mean(x * x, axis=-1, keepdims=True) + EPS)
    h = (y * g_ref[...]) * (1.0 + mod_ref[0, :, d:2 * d]) + mod_ref[0, :, 0:d]
    hb = h.astype(BF16)
    pa_ref[0] = _dot(hb, wa_ref[...])
    pb_ref[0] = _dot(hb, wb_ref[...])
    pc_ref[0] = _dot(hb, wc_ref[...])


def _in_proj(x, mod, g, w_a, w_b, w_c, tile):
    bsz, seq, d = x.shape
    na, nb, nc = w_a.shape[1], w_b.shape[1], w_c.shape[1]
    full = lambda b, t: (0, 0)
    return pl.pallas_call(
        functools.partial(_in_proj_kernel, d=d),
        grid=(bsz, seq // tile),
        in_specs=[
            pl.BlockSpec((1, tile, d), lambda b, t: (b, t, 0)),
            pl.BlockSpec((1, 1, mod.shape[-1]), lambda b, t: (b, 0, 0)),
            pl.BlockSpec((1, d), full),
            pl.BlockSpec((d, na), full),
            pl.BlockSpec((d, nb), full),
            pl.BlockSpec((d, nc), full),
        ],
        out_specs=[
            pl.BlockSpec((1, tile, na), lambda b, t: (b, t, 0)),
            pl.BlockSpec((1, tile, nb), lambda b, t: (b, t, 0)),
            pl.BlockSpec((1, tile, nc), lambda b, t: (b, t, 0)),
        ],
        out_shape=[
            jax.ShapeDtypeStruct((bsz, seq, na), F32),
            jax.ShapeDtypeStruct((bsz, seq, nb), F32),
            jax.ShapeDtypeStruct((bsz, seq, nc), F32),
        ],
        compiler_params=_params(("arbitrary", "arbitrary")),
        name="in_proj",
    )(x, mod, g, w_a, w_b, w_c)


def _sgu_kernel(p_ref, lng_ref, lnb_ref, w_ref, bias_ref, o_ref, *, d_a, n_heads, n_chunks):
    z = jax.nn.gelu(p_ref[0])
    u = z[:, :d_a]
    v = z[:, d_a:]
    mu = jnp.mean(v, axis=-1, keepdims=True)
    var = jnp.mean(jnp.square(v - mu), axis=-1, keepdims=True)
    vn = ((v - mu) * lax.rsqrt(var + LN_EPS) * lng_ref[...] + lnb_ref[...]).astype(BF16)
    rows = []
    for c in range(n_chunks):
        vc = vn[c * SGU_CHUNK:(c + 1) * SGU_CHUNK]
        cols = [_dot(w_ref[h], vc[:, h * HEAD_DIM:(h + 1) * HEAD_DIM]) for h in range(n_heads)]
        rows.append(jnp.concatenate(cols, axis=1) + bias_ref[...])
    mixed = jnp.concatenate(rows, axis=0) if n_chunks > 1 else rows[0]
    o_ref[0] = (u * mixed).astype(o_ref.dtype)


def _sgu(p_a, ln_g, ln_b, w_masked, bias_cols, tile):
    bsz, seq, two_da = p_a.shape
    d_a = two_da // 2
    n_heads = d_a // HEAD_DIM
    full2 = lambda b, t: (0, 0)
    return pl.pallas_call(
        functools.partial(_sgu_kernel, d_a=d_a, n_heads=n_heads, n_chunks=tile // SGU_CHUNK),
        grid=(bsz, seq // tile),
        in_specs=[
            pl.BlockSpec((1, tile, two_da), lambda b, t: (b, t, 0)),
            pl.BlockSpec((1, d_a), full2),
            pl.BlockSpec((1, d_a), full2),
            pl.BlockSpec((n_heads, SGU_CHUNK, SGU_CHUNK), lambda b, t: (0, 0, 0)),
            pl.BlockSpec((SGU_CHUNK, d_a), full2),
        ],
        out_specs=pl.BlockSpec((1, tile, d_a), lambda b, t: (b, t, 0)),
        out_shape=jax.ShapeDtypeStruct((bsz, seq, d_a), BF16),
        compiler_params=_params(("arbitrary", "arbitrary")),
        name="sgu",
    )(p_a, ln_g, ln_b, w_masked, bias_cols)


def _rglru_kernel(p_ref, cw_ref, cb_ref, wbd_ref, bra_ref, bix_ref, lam_ref, o_ref,
                  xbuf, hprev, *, d_b, tile):
    t = pl.program_id(1)

    @pl.when(t == 0)
    def _():
        xbuf[0:SUBLANES, :] = jnp.zeros((SUBLANES, d_b), F32)
        hprev[...] = jnp.zeros_like(hprev)

    xbuf[SUBLANES:SUBLANES + tile, :] = p_ref[0, :, 0:d_b]
    yg = p_ref[0, :, d_b:2 * d_b]
    xr = cb_ref[...] + cw_ref[0:1, :] * xbuf[pl.ds(SUBLANES - (CONV_B - 1), tile), :]
    for j in range(1, CONV_B):
        xr = xr + cw_ref[j:j + 1, :] * xbuf[pl.ds(SUBLANES - (CONV_B - 1) + j, tile), :]
    xbuf[0:SUBLANES, :] = xbuf[tile:tile + SUBLANES, :]

    ri = _dot(xr.astype(BF16), wbd_ref[...])
    r = jax.nn.sigmoid(ri[:, 0:d_b] + bra_ref[...])
    i = jax.nn.sigmoid(ri[:, d_b:2 * d_b] + bix_ref[...])
    log_a = (-LRU_C * r) * _softplus(-lam_ref[...])
    a = jnp.exp(log_a)
    one_minus_a2 = -jnp.tanh(log_a) * (jnp.exp(2.0 * log_a) + 1.0)
    bterm = jnp.sqrt(one_minus_a2) * (i * xr)

    row = lax.broadcasted_iota(jnp.int32, (tile, d_b), 0)
    shift = 1
    while shift < tile:
        a_s = jnp.where(row >= shift, pltpu.roll(a, shift, 0), 1.0)
        b_s = jnp.where(row >= shift, pltpu.roll(bterm, shift, 0), 0.0)
        bterm = a * b_s + bterm
        a = a * a_s
        shift *= 2
    h = bterm + a * hprev[...]
    hprev[...] = h[tile - 1:tile, :]
    o_ref[0] = (jax.nn.gelu(yg) * h).astype(o_ref.dtype)


def _rglru(p_b, conv_w, conv_b, w_bd, b_ra, b_ix, lam, tile):
    bsz, seq, two_db = p_b.shape
    d_b = two_db // 2
    full2 = lambda b, t: (0, 0)
    return pl.pallas_call(
        functools.partial(_rglru_kernel, d_b=d_b, tile=tile),
        grid=(bsz, seq // tile),
        in_specs=[
            pl.BlockSpec((1, tile, two_db), lambda b, t: (b, t, 0)),
            pl.BlockSpec((CONV_B, d_b), full2),
            pl.BlockSpec((1, d_b), full2),
            pl.BlockSpec((d_b, 2 * d_b), full2),
            pl.BlockSpec((1, d_b), full2),
            pl.BlockSpec((1, d_b), full2),
            pl.BlockSpec((1, d_b), full2),
        ],
        out_specs=pl.BlockSpec((1, tile, d_b), lambda b, t: (b, t, 0)),
        out_shape=jax.ShapeDtypeStruct((bsz, seq, d_b), BF16),
        scratch_shapes=[
            pltpu.VMEM((tile + SUBLANES, d_b), F32),
            pltpu.VMEM((1, d_b), F32),
        ],
        compiler_params=_params(("arbitrary", "arbitrary")),
        name="rglru",
    )(p_b, conv_w, conv_b, w_bd, b_ra, b_ix, lam)


def _stack_pair(x, lane_lo):
    return jnp.concatenate([jnp.where(lane_lo, x, 0.0), jnp.where(lane_lo, 0.0, x)], axis=0)


def _unstack_pair(x):
    return x[0:RWKV_CHUNK] + x[RWKV_CHUNK:2 * RWKV_CHUNK]


def _rwkv_chunk_kernel(p_ref, mu_ref, w0_ref, a0_ref, kk_ref, ka_ref, rk_ref, wl_ref, g2_ref, ones_ref,
                       rbar_ref, mmat_ref, obar_ref, gbar_ref, gate_ref, bonus_ref,
                       xbuf, kt_s, rt_s, kh_s, bh_s, khl_s, bhl_s, v_s, gl_s, *, d_c, tile):
    t = pl.program_id(1)
    p_c = p_ref.shape[-1]
    n_chunks = tile // RWKV_CHUNK
    n_pairs = d_c // LANES

    @pl.when(t == 0)
    def _():
        xbuf[0:SUBLANES, :] = jnp.zeros((SUBLANES, p_c), F32)

    xbuf[SUBLANES:SUBLANES + tile, :] = p_ref[0]
    p = p_ref[0]
    prev = xbuf[pl.ds(SUBLANES - 1, tile), :]
    xbuf[0:SUBLANES, :] = xbuf[tile:tile + SUBLANES, :]
    ps = p + (prev - p) * mu_ref[...]

    r = ps[:, 0:d_c]
    k = ps[:, d_c:2 * d_c]
    v = ps[:, 2 * d_c:3 * d_c]
    xwa = ps[:, 3 * d_c:3 * d_c + LORA_W + LORA_A]
    xg = ps[:, 3 * d_c + LORA_W + LORA_A:]

    lane_l = lax.broadcasted_iota(jnp.int32, xwa.shape, 1)
    lin = jnp.where(lane_l < LORA_W, jnp.tanh(xwa), xwa).astype(BF16)
    wa_lora = _dot(lin, wl_ref[...])
    w = -_softplus(-(w0_ref[...] + wa_lora[:, 0:d_c])) - 0.5
    lw = -jnp.exp(w)
    a = jax.nn.sigmoid(a0_ref[...] + wa_lora[:, d_c:2 * d_c])
    gate_ref[0] = _dot(jax.nn.sigmoid(xg).astype(BF16), g2_ref[...])

    ones_bd = ones_ref[...]
    kk = k * kk_ref[...]
    kappa = kk * lax.rsqrt(jnp.maximum(_dot_split(kk * kk, ones_bd), 1e-24))
    kmod = k * (1.0 + (a - 1.0) * ka_ref[...])
    bvec = kappa * a
    bonus_ref[0] = _dot_split(r * kmod * rk_ref[...], ones_bd) * v

    ri = lax.broadcasted_iota(jnp.int32, (tile, tile), 0)
    ci = lax.broadcasted_iota(jnp.int32, (tile, tile), 1)
    shift = RWKV_CHUNK.bit_length() - 1
    same = jnp.right_shift(ri, shift) == jnp.right_shift(ci, shift)
    sum_m = jnp.concatenate([jnp.where(same & (ci <= ri), 1.0, 0.0), jnp.where(same, 1.0, 0.0)],
                            axis=0).astype(BF16)
    lw_hi, lw_lo = _split(lw)
    lw_lo2 = (lw - lw_hi.astype(F32) - lw_lo.astype(F32)).astype(BF16)
    sums = _dot(sum_m, lw_hi) + _dot(sum_m, lw_lo) + _dot(sum_m, lw_lo2)
    c_incl = sums[0:tile]
    c_tot = sums[tile:2 * tile]

    e_neg = jnp.exp(-c_incl)
    e_end = jnp.exp(c_tot - c_incl)
    kt_s[...] = kappa * jnp.exp(c_incl - lw)
    rt_s[...] = r * jnp.exp(c_incl)
    kh_s[...] = kmod * e_neg
    bh_s[...] = bvec * e_neg
    khl_s[...] = kmod * e_end
    bhl_s[...] = bvec * e_end
    v_s[...] = v
    gl_s[...] = jnp.exp(c_tot)

    ii = lax.broadcasted_iota(jnp.int32, (LANES, LANES), 0)
    jj = lax.broadcasted_iota(jnp.int32, (LANES, LANES), 1)
    strict = ii > jj
    incl = ii >= jj
    eye = ii == jj
    lane_lo = lax.broadcasted_iota(jnp.int32, (RWKV_CHUNK, LANES), 1) < HEAD_DIM

    def chunk_body(c, carry):
        rows = pl.ds(pl.multiple_of(c * RWKV_CHUNK, RWKV_CHUNK), RWKV_CHUNK)
        for pi in range(n_pairs):
            lanes = slice(pi * LANES, (pi + 1) * LANES)
            kts = _stack_pair(kt_s[rows, lanes], lane_lo)
            rts = _stack_pair(rt_s[rows, lanes], lane_lo)
            khs = _stack_pair(kh_s[rows, lanes], lane_lo).astype(BF16)
            bhs = _stack_pair(bh_s[rows, lanes], lane_lo).astype(BF16)
            khls = _stack_pair(khl_s[rows, lanes], lane_lo).astype(BF16)
            bhls = _stack_pair(bhl_s[rows, lanes], lane_lo).astype(BF16)
            vbd = _stack_pair(v_s[rows, lanes], lane_lo).astype(BF16)
            kts_b = kts.astype(BF16)
            rts_b = rts.astype(BF16)

            sc = _dot_nt(jnp.concatenate([kts_b, rts_b], axis=0), jnp.concatenate([khs, bhs], axis=0))
            a_k = jnp.where(strict, sc[0:LANES, 0:LANES], 0.0)
            n_b = jnp.where(strict, sc[0:LANES, LANES:2 * LANES], 0.0)
            p_m = jnp.where(incl, sc[LANES:2 * LANES, 0:LANES], 0.0)
            q_m = jnp.where(incl, sc[LANES:2 * LANES, LANES:2 * LANES], 0.0)

            x_m = jnp.where(eye, 1.0, 0.0) - n_b
            nb16 = n_b.astype(BF16)
            pw = _dot(nb16, nb16)
            for _ in range(4):
                pwb = pw.astype(BF16)
                y = _dot(jnp.concatenate([x_m.astype(BF16), pwb], axis=0), pwb)
                x_m = x_m + y[0:LANES]
                pw = y[LANES:2 * LANES]
            t_m = (x_m + _dot(x_m.astype(BF16), pw.astype(BF16))).astype(BF16)

            av = _dot(jnp.concatenate([a_k.astype(BF16), p_m.astype(BF16)], axis=0), vbd)
            akv = av[0:LANES]
            pv = av[LANES:2 * LANES]
            ty = _dot(t_m, jnp.concatenate([kts_b, akv.astype(BF16)], axis=1))
            ty_b = ty.astype(BF16)
            qz = _dot(q_m.astype(BF16), ty_b)
            rbar = rts - qz[:, 0:LANES]
            obar = pv - qz[:, LANES:2 * LANES]
            bz = _dot_tn(bhls, ty_b)
            gl = gl_s[rows, lanes][0:1, :]
            mmat = jnp.where(eye, gl, 0.0) - bz[:, 0:LANES]
            gbar = _dot_tn(khls, vbd) - bz[:, LANES:2 * LANES]

            rbar_ref[0, rows, lanes] = _unstack_pair(rbar).astype(rbar_ref.dtype)
            mmat_ref[0, rows, lanes] = _unstack_pair(mmat).astype(mmat_ref.dtype)
            obar_ref[0, rows, lanes] = _unstack_pair(obar)
            gbar_ref[0, rows, lanes] = _unstack_pair(gbar)
        return carry

    lax.fori_loop(0, n_chunks, chunk_body, 0)


def _rwkv_chunk(p_c, mu, w0, a0, k_k, k_a, r_k, w_lora, g2, ones_bd, tile):
    bsz, seq, pc = p_c.shape
    d_c = w0.shape[-1]
    full2 = lambda b, t: (0, 0)
    tok = pl.BlockSpec((1, tile, d_c), lambda b, t: (b, t, 0))
    shp = lambda dt: jax.ShapeDtypeStruct((bsz, seq, d_c), dt)
    vec = pl.BlockSpec((1, d_c), full2)
    return pl.pallas_call(
        functools.partial(_rwkv_chunk_kernel, d_c=d_c, tile=tile),
        grid=(bsz, seq // tile),
        in_specs=[
            pl.BlockSpec((1, tile, pc), lambda b, t: (b, t, 0)),
            pl.BlockSpec((1, pc), full2),
            vec, vec, vec, vec, vec,
            pl.BlockSpec(w_lora.shape, full2),
            pl.BlockSpec(g2.shape, full2),
            pl.BlockSpec(ones_bd.shape, full2),
        ],
        out_specs=[tok, tok, tok, tok, tok, tok],
        out_shape=[shp(BF16), shp(BF16), shp(F32), shp(F32), shp(F32), shp(F32)],
        scratch_shapes=[pltpu.VMEM((tile + SUBLANES, pc), F32)] + [pltpu.VMEM((tile, d_c), F32)] * 8,
        compiler_params=_params(("arbitrary", "arbitrary")),
        name="rwkv_chunk",
    )(p_c, mu, w0, a0, k_k, k_a, r_k, w_lora, g2, ones_bd)


def _rwkv_scan_kernel(rbar_ref, mmat_ref, obar_ref, gbar_ref, gate_ref, bonus_ref, lnw_ref, lnb_ref, ones_ref,
                      y_ref, h_s, o_s, *, d_c, tile):
    t = pl.program_id(1)
    n_chunks = tile // RWKV_CHUNK
    n_pairs = d_c // LANES

    @pl.when(t == 0)
    def _():
        h_s[...] = jnp.zeros_like(h_s)

    lane_lo = lax.broadcasted_iota(jnp.int32, (RWKV_CHUNK, LANES), 1) < HEAD_DIM

    def chunk_body(c, carry):
        rows = pl.ds(pl.multiple_of(c * RWKV_CHUNK, RWKV_CHUNK), RWKV_CHUNK)
        for pi in range(n_pairs):
            lanes = slice(pi * LANES, (pi + 1) * LANES)
            hbd = _stack_pair(h_s[pi], lane_lo).astype(BF16)
            lhs = jnp.concatenate([rbar_ref[0, rows, lanes], mmat_ref[0, rows, lanes]], axis=0)
            out = _dot(lhs, hbd)
            o_s[rows, lanes] = out[0:RWKV_CHUNK] + obar_ref[0, rows, lanes]
            h_s[pi] = out[RWKV_CHUNK:2 * RWKV_CHUNK] + gbar_ref[0, rows, lanes]
        return carry

    lax.fori_loop(0, n_chunks, chunk_body, 0)

    o = o_s[...]
    mean_m = ones_ref[...]
    inv = 1.0 / HEAD_DIM
    mean = _dot_split(o, mean_m) * inv
    dlt = o - mean
    var = _dot_split(dlt * dlt, mean_m) * inv
    gn = dlt * lax.rsqrt(var + GN_EPS) * lnw_ref[...] + lnb_ref[...]
    y_ref[0] = ((gn + bonus_ref[0]) * gate_ref[0]).astype(y_ref.dtype)


def _rwkv_scan(rbar, mmat, obar, gbar, gate, bonus, ln_w, ln_b, ones_bd, tile):
    bsz, seq, d_c = obar.shape
    full2 = lambda b, t: (0, 0)
    tok = pl.BlockSpec((1, tile, d_c), lambda b, t: (b, t, 0))
    vec = pl.BlockSpec((1, d_c), full2)
    return pl.pallas_call(
        functools.partial(_rwkv_scan_kernel, d_c=d_c, tile=tile),
        grid=(bsz, seq // tile),
        in_specs=[tok, tok, tok, tok, tok, tok, vec, vec, pl.BlockSpec(ones_bd.shape, full2)],
        out_specs=tok,
        out_shape=jax.ShapeDtypeStruct((bsz, seq, d_c), BF16),
        scratch_shapes=[
            pltpu.VMEM((d_c // LANES, RWKV_CHUNK, LANES), F32),
            pltpu.VMEM((tile, d_c), F32),
        ],
        compiler_params=_params(("arbitrary", "arbitrary")),
        name="rwkv_scan",
    )(rbar, mmat, obar, gbar, gate, bonus, ln_w, ln_b, ones_bd)


def _out_ffn_kernel(x_ref, ya_ref, yb_ref, yc_ref, mod_ref, woa_ref, wob_ref, woc_ref, g_ref,
                    wg_ref, wv_ref, cw_ref, cb_ref, wd_ref, gf_ref, o_ref,
                    x1_s, h2_s, acc_s, gbuf, carry_s, *, d, tile, n_j, final_norm):
    t = pl.program_id(1)
    j = pl.program_id(2)
    fc = wg_ref.shape[-1]

    @pl.when(j == 0)
    def _():
        y = _dot(ya_ref[0], woa_ref[...]) + _dot(yb_ref[0], wob_ref[...]) + _dot(yc_ref[0], woc_ref[...])
        x1 = x_ref[0] + mod_ref[0, :, 2 * d:3 * d] * y
        x1_s[...] = x1
        n = x1 * lax.rsqrt(jnp.mean(x1 * x1, axis=-1, keepdims=True) + EPS)
        h2 = (n * g_ref[...]) * (1.0 + mod_ref[0, :, 4 * d:5 * d]) + mod_ref[0, :, 3 * d:4 * d]
        h2_s[...] = h2.astype(BF16)
        acc_s[...] = jnp.zeros_like(acc_s)

    @pl.when(t == 0)
    def _():
        carry_s[j] = jnp.zeros((SUBLANES, fc), F32)

    h2 = h2_s[...]
    gbuf[0:SUBLANES, :] = carry_s[j]
    gbuf[SUBLANES:SUBLANES + tile, :] = _dot(h2, wg_ref[...])
    val = _dot(h2, wv_ref[...])
    gc = cb_ref[...] + cw_ref[0:1, :] * gbuf[pl.ds(SUBLANES - (CONV_FF - 1), tile), :]
    for q in range(1, CONV_FF):
        gc = gc + cw_ref[q:q + 1, :] * gbuf[pl.ds(SUBLANES - (CONV_FF - 1) + q, tile), :]
    carry_s[j] = gbuf[tile:tile + SUBLANES, :]
    mid = (gc * jax.nn.sigmoid(gc) * val).astype(BF16)
    acc_s[...] += _dot(mid, wd_ref[...])

    @pl.when(j == n_j - 1)
    def _():
        x2 = x1_s[...] + mod_ref[0, :, 5 * d:6 * d] * acc_s[...]
        if final_norm:
            x2 = x2 * lax.rsqrt(jnp.mean(x2 * x2, axis=-1, keepdims=True) + EPS) * gf_ref[...]
        o_ref[0] = x2


def _out_ffn(x, y_a, y_b, y_c, mod, wo_a, wo_b, wo_c, g_ffn, w_up, conv_w, conv_b, w_down, g_final,
             tile, n_j, final_norm):
    bsz, seq, d = x.shape
    d_ff = w_down.shape[0]
    fc = d_ff // n_j
    c3 = lambda b, t, j: (0, 0)
    tok = lambda w: pl.BlockSpec((1, tile, w), lambda b, t, j: (b, t, 0))
    return pl.pallas_call(
        functools.partial(_out_ffn_kernel, d=d, tile=tile, n_j=n_j, final_norm=final_norm),
        grid=(bsz, seq // tile, n_j),
        in_specs=[
            tok(d), tok(y_a.shape[-1]), tok(y_b.shape[-1]), tok(y_c.shape[-1]),
            pl.BlockSpec((1, 1, mod.shape[-1]), lambda b, t, j: (b, 0, 0)),
            pl.BlockSpec(wo_a.shape, c3), pl.BlockSpec(wo_b.shape, c3), pl.BlockSpec(wo_c.shape, c3),
            pl.BlockSpec((1, d), c3),
            pl.BlockSpec((d, fc), lambda b, t, j: (0, j)),
            pl.BlockSpec((d, fc), lambda b, t, j: (0, n_j + j)),
            pl.BlockSpec((CONV_FF, fc), lambda b, t, j: (0, j)),
            pl.BlockSpec((1, fc), lambda b, t, j: (0, j)),
            pl.BlockSpec((fc, d), lambda b, t, j: (j, 0)),
            pl.BlockSpec((1, d), c3),
        ],
        out_specs=pl.BlockSpec((1, tile, d), lambda b, t, j: (b, t, 0)),
        out_shape=jax.ShapeDtypeStruct((bsz, seq, d), F32),
        scratch_shapes=[
            pltpu.VMEM((tile, d), F32),
            pltpu.VMEM((tile, d), BF16),
            pltpu.VMEM((tile, d), F32),
            pltpu.VMEM((tile + SUBLANES, fc), F32),
            pltpu.VMEM((n_j, SUBLANES, fc), F32),
        ],
        compiler_params=_params(("arbitrary", "arbitrary", "arbitrary")),
        name="out_ffn",
    )(x, y_a, y_b, y_c, mod, wo_a, wo_b, wo_c, g_ffn, w_up, w_up, conv_w, conv_b, w_down, g_final)


def _block_diag(w):
    h, n, _ = w.shape
    eye = jnp.eye(h, dtype=w.dtype)
    return (eye[:, None, :, None] * w[:, :, None, :]).reshape(h * n, h * n)


def _pick_tile(seq, want):
    tile = min(seq, want)
    while seq % tile:
        tile //= 2
    return tile


def kernel(x, c, w_mod, b_mod, norm_mix, w_in, w_out, sgu_ln_g, sgu_ln_b, sgu_w, sgu_b, lru_conv_w, lru_conv_b, lru_w_a, lru_b_a, lru_w_x, lru_b_x, lru_lambda, rwkv_mu, rwkv_w0, rwkv_w2, rwkv_a0, rwkv_a2, rwkv_g2, rwkv_k_k, rwkv_k_a, rwkv_r_k, rwkv_ln_w, rwkv_ln_b, norm_ffn, ffn_w_up, ffn_conv_w, ffn_conv_b, ffn_w_down, norm_final):
    bsz, seq, d = x.shape
    depth = w_in.shape[0]
    d_a = sgu_ln_g.shape[-1]
    d_b = lru_conv_b.shape[-1]
    d_c = rwkv_w0.shape[-1]
    p_a, p_b = 2 * d_a, 2 * d_b
    n_heads_c = d_c // HEAD_DIM

    t_proj = _pick_tile(seq, 512)
    t_sgu = _pick_tile(seq, 512)
    t_lru = _pick_tile(seq, 512)
    t_chunk = _pick_tile(seq, 256)
    t_scan = _pick_tile(seq, 512)
    t_ffn = _pick_tile(seq, 512)
    n_j = 2

    mod = _modulation(c, w_mod, b_mod)
    ones_bd = _block_diag(jnp.ones((n_heads_c, HEAD_DIM, HEAD_DIM), BF16))
    tril = jnp.tril(jnp.ones((SGU_CHUNK, SGU_CHUNK), dtype=bool))
    row2 = lambda v: v.reshape(1, -1)

    for l in range(depth):
        mod_l = mod[l].reshape(bsz, 1, N_MOD * d)
        w_in_l = w_in[l].astype(BF16)
        pa, pb, pc = _in_proj(x, mod_l, row2(norm_mix[l]), w_in_l[:, :p_a], w_in_l[:, p_a:p_a + p_b],
                              w_in_l[:, p_a + p_b:], t_proj)

        sgu_wm = jnp.where(tril, sgu_w[l], 0.0).astype(BF16)
        sgu_bias = jnp.repeat(sgu_b[l].T, HEAD_DIM, axis=1)
        y_a = _sgu(pa, row2(sgu_ln_g[l]), row2(sgu_ln_b[l]), sgu_wm, sgu_bias, t_sgu)

        w_bd = jnp.concatenate([_block_diag(lru_w_a[l]), _block_diag(lru_w_x[l])], axis=1).astype(BF16)
        y_b = _rglru(pb, lru_conv_w[l], row2(lru_conv_b[l]), w_bd, row2(lru_b_a[l]), row2(lru_b_x[l]),
                     row2(lru_lambda[l]), t_lru)

        zeros_l = jnp.zeros((LORA_W, d_c), F32)
        w_lora = jnp.concatenate([
            jnp.concatenate([rwkv_w2[l], zeros_l], axis=1),
            jnp.concatenate([zeros_l, rwkv_a2[l]], axis=1)], axis=0).astype(BF16)
        rbar, mmat, obar, gbar, gate, bonus = _rwkv_chunk(
            pc, row2(rwkv_mu[l]), row2(rwkv_w0[l]), row2(rwkv_a0[l]), row2(rwkv_k_k[l]), row2(rwkv_k_a[l]),
            row2(rwkv_r_k[l]), w_lora, rwkv_g2[l].astype(BF16), ones_bd, t_chunk)
        y_c = _rwkv_scan(rbar, mmat, obar, gbar, gate, bonus, row2(rwkv_ln_w[l]), row2(rwkv_ln_b[l]),
                         ones_bd, t_scan)

        w_out_l = w_out[l].astype(BF16)
        x = _out_ffn(x, y_a, y_b, y_c, mod_l, w_out_l[:d_a], w_out_l[d_a:d_a + d_b], w_out_l[d_a + d_b:],
                     row2(norm_ffn[l]), ffn_w_up[l].astype(BF16), ffn_conv_w[l], row2(ffn_conv_b[l]),
                     ffn_w_down[l].astype(BF16), row2(norm_final), t_ffn, n_j, l == depth - 1)
    return x
```

```python
import functools

import jax
import jax.numpy as jnp
from jax import lax
from jax.experimental import pallas as pl
from jax.experimental.pallas import tpu as pltpu

HEAD_DIM = 64
SGU_CHUNK = 128
CONV_B = 4
LRU_C = 8.0
LORA_W = 64
LORA_A = 64
LORA_G = 128
CONV_FF = 3
N_MOD = 6
EPS = 1e-6
LN_EPS = 1e-5
GN_EPS = 64e-5

RWKV_CHUNK = 64
LANES = 128
SUBLANES = 8
VMEM_LIMIT = 56 * 1024 * 1024

BF16 = jnp.bfloat16
F32 = jnp.float32


def _dot(a, b):
    return jnp.dot(a, b, preferred_element_type=F32)


def _dot_nt(a, b):
    return lax.dot_general(a, b, (((1,), (1,)), ((), ())), preferred_element_type=F32)


def _dot_tn(a, b):
    return lax.dot_general(a, b, (((0,), (0,)), ((), ())), preferred_element_type=F32)


def _split(x):
    hi = x.astype(BF16)
    lo = (x - hi.astype(F32)).astype(BF16)
    return hi, lo


def _head_sums(x, ones2):
    outs = []
    for pi in range(x.shape[-1] // LANES):
        hi, lo = _split(x[:, pi * LANES:(pi + 1) * LANES])
        outs.append(_dot(jnp.concatenate([hi, lo], axis=1), ones2))
    return jnp.concatenate(outs, axis=1)


def _softplus(z):
    return jnp.maximum(z, 0.0) + jnp.log1p(jnp.exp(-jnp.abs(z)))


def _params(sem):
    return pltpu.CompilerParams(dimension_semantics=sem, vmem_limit_bytes=VMEM_LIMIT)


def _mod_kernel(c_ref, w_ref, b_ref, o_ref):
    c = c_ref[...]
    ca = c * jax.nn.sigmoid(c)
    w = w_ref[0]
    chi, clo = _split(ca)
    whi, wlo = _split(w)
    acc = _dot(chi, whi) + _dot(chi, wlo) + _dot(clo, whi)
    o_ref[0] = acc + b_ref[0]


def _modulation(c, w_mod, b_mod):
    depth, d, nd = w_mod.shape
    bsz = c.shape[0]
    nblk = nd // d
    return pl.pallas_call(
        _mod_kernel,
        grid=(depth, nblk),
        in_specs=[
            pl.BlockSpec((bsz, d), lambda l, j: (0, 0)),
            pl.BlockSpec((1, d, d), lambda l, j: (l, 0, j)),
            pl.BlockSpec((1, 1, d), lambda l, j: (l, 0, j)),
        ],
        out_specs=pl.BlockSpec((1, bsz, d), lambda l, j: (l, 0, j)),
        out_shape=jax.ShapeDtypeStruct((depth, bsz, nd), F32),
        compiler_params=_params(("arbitrary", "arbitrary")),
        name="adaln_mod",
    )(c, w_mod, b_mod.reshape(depth, 1, nd))


def _in_proj_kernel(x_ref, mod_ref, g_ref, wa_ref, wb_ref, wc_ref, pa_ref, pb_ref, pc_ref, *, d):
    x = x_ref[0]
    y = x * lax.rsqrt(jnp.mean(x * x, axis=-1, keepdims=True) + EPS)
    h = (y * g_ref[...]) * (1.0 + mod_ref[0, :, d:2 * d]) + mod_ref[0, :, 0:d]
    hb = h.astype(BF16)
    pa_ref[0] = _dot(hb, wa_ref[...])
    pb_ref[0] = _dot(hb, wb_ref[...])
    pc_ref[0] = _dot(hb, wc_ref[...])


def _in_proj(x, mod, g, w_a, w_b, w_c, tile):
    bsz, seq, d = x.shape
    na, nb, nc = w_a.shape[1], w_b.shape[1], w_c.shape[1]
    full = lambda b, t: (0, 0)
    return pl.pallas_call(
        functools.partial(_in_proj_kernel, d=d),
        grid=(bsz, seq // tile),
        in_specs=[
            pl.BlockSpec((1, tile, d), lambda b, t: (b, t, 0)),
            pl.BlockSpec((1, 1, mod.shape[-1]), lambda b, t: (b, 0, 0)),
            pl.BlockSpec((1, d), full),
            pl.BlockSpec((d, na), full),
            pl.BlockSpec((d, nb), full),
            pl.BlockSpec((d, nc), full),
        ],
        out_specs=[
            pl.BlockSpec((1, tile, na), lambda b, t: (b, t, 0)),
            pl.BlockSpec((1, tile, nb), lambda b, t: (b, t, 0)),
            pl.BlockSpec((1, tile, nc), lambda b, t: (b, t, 0)),
        ],
        out_shape=[
            jax.ShapeDtypeStruct((bsz, seq, na), F32),
            jax.ShapeDtypeStruct((bsz, seq, nb), F32),
            jax.ShapeDtypeStruct((bsz, seq, nc), F32),
        ],
        compiler_params=_params(("arbitrary", "arbitrary")),
        name="in_proj",
    )(x, mod, g, w_a, w_b, w_c)


def _sgu_kernel(p_ref, lng_ref, lnb_ref, w_ref, bias_ref, o_ref, *, d_a, n_heads, n_chunks):
    z = jax.nn.gelu(p_ref[0])
    u = z[:, :d_a]
    v = z[:, d_a:]
    mu = jnp.mean(v, axis=-1, keepdims=True)
    var = jnp.mean(jnp.square(v - mu), axis=-1, keepdims=True)
    vn = ((v - mu) * lax.rsqrt(var + LN_EPS) * lng_ref[...] + lnb_ref[...]).astype(BF16)
    rows = []
    for c in range(n_chunks):
        vc = vn[c * SGU_CHUNK:(c + 1) * SGU_CHUNK]
        cols = [_dot(w_ref[h], vc[:, h * HEAD_DIM:(h + 1) * HEAD_DIM]) for h in range(n_heads)]
        rows.append(jnp.concatenate(cols, axis=1) + bias_ref[...])
    mixed = jnp.concatenate(rows, axis=0) if n_chunks > 1 else rows[0]
    o_ref[0] = (u * mixed).astype(o_ref.dtype)


def _sgu(p_a, ln_g, ln_b, w_masked, bias_cols, tile):
    bsz, seq, two_da = p_a.shape
    d_a = two_da // 2
    n_heads = d_a // HEAD_DIM
    full2 = lambda b, t: (0, 0)
    return pl.pallas_call(
        functools.partial(_sgu_kernel, d_a=d_a, n_heads=n_heads, n_chunks=tile // SGU_CHUNK),
        grid=(bsz, seq // tile),
        in_specs=[
            pl.BlockSpec((1, tile, two_da), lambda b, t: (b, t, 0)),
            pl.BlockSpec((1, d_a), full2),
            pl.BlockSpec((1, d_a), full2),
            pl.BlockSpec((n_heads, SGU_CHUNK, SGU_CHUNK), lambda b, t: (0, 0, 0)),
            pl.BlockSpec((SGU_CHUNK, d_a), full2),
        ],
        out_specs=pl.BlockSpec((1, tile, d_a), lambda b, t: (b, t, 0)),
        out_shape=jax.ShapeDtypeStruct((bsz, seq, d_a), BF16),
        compiler_params=_params(("arbitrary", "arbitrary")),
        name="sgu",
    )(p_a, ln_g, ln_b, w_masked, bias_cols)


def _rglru_kernel(p_ref, cw_ref, cb_ref, wbd_ref, bra_ref, bix_ref, lam_ref, o_ref,
                  xbuf, hprev, *, d_b, tile):
    t = pl.program_id(1)

    @pl.when(t == 0)
    def _():
        xbuf[0:SUBLANES, :] = jnp.zeros((SUBLANES, d_b), F32)
        hprev[...] = jnp.zeros_like(hprev)

    xbuf[SUBLANES:SUBLANES + tile, :] = p_ref[0, :, 0:d_b]
    yg = p_ref[0, :, d_b:2 * d_b]
    xr = cb_ref[...] + cw_ref[0:1, :] * xbuf[pl.ds(SUBLANES - (CONV_B - 1), tile), :]
    for j in range(1, CONV_B):
        xr = xr + cw_ref[j:j + 1, :] * xbuf[pl.ds(SUBLANES - (CONV_B - 1) + j, tile), :]
    xbuf[0:SUBLANES, :] = xbuf[tile:tile + SUBLANES, :]

    ri = _dot(xr.astype(BF16), wbd_ref[...])
    r = jax.nn.sigmoid(ri[:, 0:d_b] + bra_ref[...])
    i = jax.nn.sigmoid(ri[:, d_b:2 * d_b] + bix_ref[...])
    log_a = (-LRU_C * r) * _softplus(-lam_ref[...])
    a = jnp.exp(log_a)
    one_minus_a2 = -jnp.tanh(log_a) * (jnp.exp(2.0 * log_a) + 1.0)
    bterm = jnp.sqrt(one_minus_a2) * (i * xr)

    n_grp = tile // SUBLANES
    a3 = a.reshape(n_grp, SUBLANES, d_b)
    b3 = bterm.reshape(n_grp, SUBLANES, d_b)
    sub = lax.broadcasted_iota(jnp.int32, (n_grp, SUBLANES, d_b), 1)
    shift = 1
    while shift < SUBLANES:
        keep = sub >= shift
        a_s = jnp.where(keep, pltpu.roll(a3, shift, 1), 1.0)
        b_s = jnp.where(keep, pltpu.roll(b3, shift, 1), 0.0)
        b3 = a3 * b_s + b3
        a3 = a3 * a_s
        shift *= 2
    h = hprev[...]
    groups = []
    for g in range(n_grp):
        hg = b3[g] + a3[g] * h
        groups.append(hg)
        h = hg[SUBLANES - 1:SUBLANES, :]
    hprev[...] = h
    o_ref[0] = (jax.nn.gelu(yg) * jnp.concatenate(groups, axis=0)).astype(o_ref.dtype)


def _rglru(p_b, conv_w, conv_b, w_bd, b_ra, b_ix, lam, tile):
    bsz, seq, two_db = p_b.shape
    d_b = two_db // 2
    full2 = lambda b, t: (0, 0)
    return pl.pallas_call(
        functools.partial(_rglru_kernel, d_b=d_b, tile=tile),
        grid=(bsz, seq // tile),
        in_specs=[
            pl.BlockSpec((1, tile, two_db), lambda b, t: (b, t, 0)),
            pl.BlockSpec((CONV_B, d_b), full2),
            pl.BlockSpec((1, d_b), full2),
            pl.BlockSpec((d_b, 2 * d_b), full2),
            pl.BlockSpec((1, d_b), full2),
            pl.BlockSpec((1, d_b), full2),
            pl.BlockSpec((1, d_b), full2),
        ],
        out_specs=pl.BlockSpec((1, tile, d_b), lambda b, t: (b, t, 0)),
        out_shape=jax.ShapeDtypeStruct((bsz, seq, d_b), BF16),
        scratch_shapes=[
            pltpu.VMEM((tile + SUBLANES, d_b), F32),
            pltpu.VMEM((1, d_b), F32),
        ],
        compiler_params=_params(("arbitrary", "arbitrary")),
        name="rglru",
    )(p_b, conv_w, conv_b, w_bd, b_ra, b_ix, lam)


def _stack_pair(x, lane_lo):
    return jnp.concatenate([jnp.where(lane_lo, x, 0.0), jnp.where(lane_lo, 0.0, x)], axis=0)


def _unstack_pair(x):
    return x[0:RWKV_CHUNK] + x[RWKV_CHUNK:2 * RWKV_CHUNK]


def _rwkv_chunk_kernel(p_ref, mu_ref, w0_ref, a0_ref, kk_ref, ka_ref, rk_ref, wl_ref, g2_ref, ones_ref,
                       rbar_ref, mmat_ref, obar_ref, gbar_ref, gate_ref, bonus_ref,
                       xbuf, *, d_c, tile):
    t = pl.program_id(1)
    p_c = p_ref.shape[-1]
    n_chunks = tile // RWKV_CHUNK
    n_pairs = d_c // LANES

    @pl.when(t == 0)
    def _():
        xbuf[0:SUBLANES, :] = jnp.zeros((SUBLANES, p_c), F32)

    xbuf[SUBLANES:SUBLANES + tile, :] = p_ref[0]
    p = p_ref[0]
    prev = xbuf[pl.ds(SUBLANES - 1, tile), :]
    xbuf[0:SUBLANES, :] = xbuf[tile:tile + SUBLANES, :]
    ps = p + (prev - p) * mu_ref[...]

    r = ps[:, 0:d_c]
    k = ps[:, d_c:2 * d_c]
    v = ps[:, 2 * d_c:3 * d_c]
    xwa = ps[:, 3 * d_c:3 * d_c + LORA_W + LORA_A]
    xg = ps[:, 3 * d_c + LORA_W + LORA_A:]

    lane_l = lax.broadcasted_iota(jnp.int32, xwa.shape, 1)
    lin = jnp.where(lane_l < LORA_W, jnp.tanh(xwa), xwa).astype(BF16)
    wa_lora = _dot(lin, wl_ref[...])
    w = -_softplus(-(w0_ref[...] + wa_lora[:, 0:d_c])) - 0.5
    lw = -jnp.exp(w)
    a = jax.nn.sigmoid(a0_ref[...] + wa_lora[:, d_c:2 * d_c])
    gate_ref[0] = _dot(jax.nn.sigmoid(xg).astype(BF16), g2_ref[...])

    ones2 = ones_ref[...]
    kk = k * kk_ref[...]
    kappa = kk * lax.rsqrt(jnp.maximum(_head_sums(kk * kk, ones2), 1e-24))
    kmod = k * (1.0 + (a - 1.0) * ka_ref[...])
    bvec = kappa * a
    bonus_ref[0] = _head_sums(r * kmod * rk_ref[...], ones2) * v

    ri = lax.broadcasted_iota(jnp.int32, (LANES, 2 * LANES), 0)
    ci = lax.broadcasted_iota(jnp.int32, (LANES, 2 * LANES), 1) & (LANES - 1)
    shift = RWKV_CHUNK.bit_length() - 1
    tril2 = jnp.where((jnp.right_shift(ri, shift) == jnp.right_shift(ci, shift)) & (ci <= ri),
                      1.0, 0.0).astype(BF16)
    lw_hi, lw_lo = _split(lw)
    c_parts = []
    for m in range(tile // LANES):
        rws = slice(m * LANES, (m + 1) * LANES)
        c_parts.append(_dot(tril2, jnp.concatenate([lw_hi[rws], lw_lo[rws]], axis=0)))
    c_incl = jnp.concatenate(c_parts, axis=0) if len(c_parts) > 1 else c_parts[0]
    c_tot = jnp.concatenate(
        [jnp.broadcast_to(c_incl[(c + 1) * RWKV_CHUNK - 1:(c + 1) * RWKV_CHUNK, :], (RWKV_CHUNK, d_c))
         for c in range(n_chunks)], axis=0)

    e_neg = jnp.exp(-c_incl)
    e_end = jnp.exp(c_tot - c_incl)
    kt = kappa * jnp.exp(c_incl - lw)
    rt = r * jnp.exp(c_incl)
    kh = kmod * e_neg
    bh = bvec * e_neg
    khl = kmod * e_end
    bhl = bvec * e_end
    gl = jnp.exp(c_tot)

    ii = lax.broadcasted_iota(jnp.int32, (LANES, LANES), 0)
    jj = lax.broadcasted_iota(jnp.int32, (LANES, LANES), 1)
    strict = ii > jj
    incl = ii >= jj
    eye = ii == jj
    lane_lo = lax.broadcasted_iota(jnp.int32, (RWKV_CHUNK, LANES), 1) < HEAD_DIM

    chains = [(c, pi) for c in range(n_chunks) for pi in range(n_pairs)]

    def rows_of(ch):
        return slice(ch[0] * RWKV_CHUNK, (ch[0] + 1) * RWKV_CHUNK)

    def lanes_of(ch):
        return slice(ch[1] * LANES, (ch[1] + 1) * LANES)

    def stacked(x, ch):
        return _stack_pair(x[rows_of(ch), lanes_of(ch)], lane_lo)

    rts = [stacked(rt, ch) for ch in chains]
    kts_b = [stacked(kt, ch).astype(BF16) for ch in chains]
    vbd = [stacked(v, ch).astype(BF16) for ch in chains]

    sc = [_dot_nt(jnp.concatenate([kts_b[n], rts[n].astype(BF16)], axis=0),
                  jnp.concatenate([stacked(kh, ch).astype(BF16), stacked(bh, ch).astype(BF16)], axis=0))
          for n, ch in enumerate(chains)]
    a_k = [jnp.where(strict, s[0:LANES, 0:LANES], 0.0).astype(BF16) for s in sc]
    n_b = [jnp.where(strict, s[0:LANES, LANES:2 * LANES], 0.0) for s in sc]
    p_m = [jnp.where(incl, s[LANES:2 * LANES, 0:LANES], 0.0).astype(BF16) for s in sc]
    q_m = [jnp.where(incl, s[LANES:2 * LANES, LANES:2 * LANES], 0.0).astype(BF16) for s in sc]

    x_m = [jnp.where(eye, 1.0, 0.0) - n for n in n_b]
    pw = [_dot(n.astype(BF16), n.astype(BF16)) for n in n_b]
    for _ in range(4):
        y = [_dot(jnp.concatenate([x.astype(BF16), p.astype(BF16)], axis=0), p.astype(BF16))
             for x, p in zip(x_m, pw)]
        x_m = [x + yy[0:LANES] for x, yy in zip(x_m, y)]
        pw = [yy[LANES:2 * LANES] for yy in y]
    t_m = [(x + _dot(x.astype(BF16), p.astype(BF16))).astype(BF16) for x, p in zip(x_m, pw)]

    av = [_dot(jnp.concatenate([a, p], axis=0), vv) for a, p, vv in zip(a_k, p_m, vbd)]
    ty_b = [_dot(tm, jnp.concatenate([kb, a[0:LANES].astype(BF16)], axis=1)).astype(BF16)
            for tm, kb, a in zip(t_m, kts_b, av)]
    qz = [_dot(q, tyb) for q, tyb in zip(q_m, ty_b)]
    bz = [_dot_tn(stacked(bhl, ch).astype(BF16), tyb) for ch, tyb in zip(chains, ty_b)]
    kv = [_dot_tn(stacked(khl, ch).astype(BF16), vv) for ch, vv in zip(chains, vbd)]

    for n, ch in enumerate(chains):
        rows, lanes = rows_of(ch), lanes_of(ch)
        rbar = rts[n] - qz[n][:, 0:LANES]
        obar = av[n][LANES:2 * LANES] - qz[n][:, LANES:2 * LANES]
        gl_row = gl[ch[0] * RWKV_CHUNK:ch[0] * RWKV_CHUNK + 1, lanes]
        mmat = jnp.where(eye, gl_row, 0.0) - bz[n][:, 0:LANES]
        gbar = kv[n] - bz[n][:, LANES:2 * LANES]
        rbar_ref[0, rows, lanes] = _unstack_pair(rbar).astype(rbar_ref.dtype)
        mmat_ref[0, rows, lanes] = _unstack_pair(mmat).astype(mmat_ref.dtype)
        obar_ref[0, rows, lanes] = _unstack_pair(obar)
        gbar_ref[0, rows, lanes] = _unstack_pair(gbar)


def _rwkv_chunk(p_c, mu, w0, a0, k_k, k_a, r_k, w_lora, g2, ones2, tile):
    bsz, seq, pc = p_c.shape
    d_c = w0.shape[-1]
    full2 = lambda b, t: (0, 0)
    tok = pl.BlockSpec((1, tile, d_c), lambda b, t: (b, t, 0))
    shp = lambda dt: jax.ShapeDtypeStruct((bsz, seq, d_c), dt)
    vec = pl.BlockSpec((1, d_c), full2)
    return pl.pallas_call(
        functools.partial(_rwkv_chunk_kernel, d_c=d_c, tile=tile),
        grid=(bsz, seq // tile),
        in_specs=[
            pl.BlockSpec((1, tile, pc), lambda b, t: (b, t, 0)),
            pl.BlockSpec((1, pc), full2),
            vec, vec, vec, vec, vec,
            pl.BlockSpec(w_lora.shape, full2),
            pl.BlockSpec(g2.shape, full2),
            pl.BlockSpec(ones2.shape, full2),
        ],
        out_specs=[tok, tok, tok, tok, tok, tok],
        out_shape=[shp(BF16), shp(BF16), shp(F32), shp(F32), shp(F32), shp(F32)],
        scratch_shapes=[pltpu.VMEM((tile + SUBLANES, pc), F32)],
        compiler_params=_params(("arbitrary", "arbitrary")),
        name="rwkv_chunk",
    )(p_c, mu, w0, a0, k_k, k_a, r_k, w_lora, g2, ones2)


def _rwkv_scan_kernel(rbar_ref, mmat_ref, obar_ref, gbar_ref, gate_ref, bonus_ref, lnw_ref, lnb_ref, ones_ref,
                      y_ref, h_s, o_s, *, d_c, tile):
    t = pl.program_id(0)
    bsz = obar_ref.shape[0]
    n_chunks = tile // RWKV_CHUNK
    n_pairs = d_c // LANES

    @pl.when(t == 0)
    def _():
        h_s[...] = jnp.zeros_like(h_s)

    lane_lo = lax.broadcasted_iota(jnp.int32, (RWKV_CHUNK, LANES), 1) < HEAD_DIM
    chains = [(b, pi) for b in range(bsz) for pi in range(n_pairs)]

    for c in range(n_chunks):
        rows = slice(c * RWKV_CHUNK, (c + 1) * RWKV_CHUNK)
        outs = []
        for n, (b, pi) in enumerate(chains):
            lanes = slice(pi * LANES, (pi + 1) * LANES)
            hbd = _stack_pair(h_s[n], lane_lo).astype(BF16)
            lhs = jnp.concatenate([rbar_ref[b, rows, lanes], mmat_ref[b, rows, lanes]], axis=0)
            outs.append(_dot(lhs, hbd))
        for n, (b, pi) in enumerate(chains):
            lanes = slice(pi * LANES, (pi + 1) * LANES)
            o_s[b, rows, lanes] = outs[n][0:RWKV_CHUNK] + obar_ref[b, rows, lanes]
            h_s[n] = outs[n][RWKV_CHUNK:2 * RWKV_CHUNK] + gbar_ref[b, rows, lanes]

    ones2 = ones_ref[...]
    inv = 1.0 / HEAD_DIM
    for b in range(bsz):
        o = o_s[b]
        mean = _head_sums(o, ones2) * inv
        dlt = o - mean
        var = _head_sums(dlt * dlt, ones2) * inv
        gn = dlt * lax.rsqrt(var + GN_EPS) * lnw_ref[...] + lnb_ref[...]
        y_ref[b] = ((gn + bonus_ref[b]) * gate_ref[b]).astype(y_ref.dtype)


def _rwkv_scan(rbar, mmat, obar, gbar, gate, bonus, ln_w, ln_b, ones2, tile):
    bsz, seq, d_c = obar.shape
    full2 = lambda t: (0, 0)
    tok = pl.BlockSpec((bsz, tile, d_c), lambda t: (0, t, 0))
    vec = pl.BlockSpec((1, d_c), full2)
    return pl.pallas_call(
        functools.partial(_rwkv_scan_kernel, d_c=d_c, tile=tile),
        grid=(seq // tile,),
        in_specs=[tok, tok, tok, tok, tok, tok, vec, vec, pl.BlockSpec(ones2.shape, full2)],
        out_specs=tok,
        out_shape=jax.ShapeDtypeStruct((bsz, seq, d_c), BF16),
        scratch_shapes=[
            pltpu.VMEM((bsz * (d_c // LANES), RWKV_CHUNK, LANES), F32),
            pltpu.VMEM((bsz, tile, d_c), F32),
        ],
        compiler_params=_params(("arbitrary",)),
        name="rwkv_scan",
    )(rbar, mmat, obar, gbar, gate, bonus, ln_w, ln_b, ones2)


def _out_ffn_kernel(x_ref, ya_ref, yb_ref, yc_ref, mod_ref, woa_ref, wob_ref, woc_ref, g_ref,
                    wg_ref, wv_ref, cw_ref, cb_ref, wd_ref, gf_ref, o_ref,
                    x1_s, h2_s, acc_s, gbuf, carry_s, *, d, tile, n_j, final_norm):
    t = pl.program_id(1)
    j = pl.program_id(2)
    fc = wg_ref.shape[-1]

    @pl.when(j == 0)
    def _():
        y = _dot(ya_ref[0], woa_ref[...]) + _dot(yb_ref[0], wob_ref[...]) + _dot(yc_ref[0], woc_ref[...])
        x1 = x_ref[0] + mod_ref[0, :, 2 * d:3 * d] * y
        x1_s[...] = x1
        n = x1 * lax.rsqrt(jnp.mean(x1 * x1, axis=-1, keepdims=True) + EPS)
        h2 = (n * g_ref[...]) * (1.0 + mod_ref[0, :, 4 * d:5 * d]) + mod_ref[0, :, 3 * d:4 * d]
        h2_s[...] = h2.astype(BF16)
        acc_s[...] = jnp.zeros_like(acc_s)

    @pl.when(t == 0)
    def _():
        carry_s[j] = jnp.zeros((SUBLANES, fc), F32)

    h2 = h2_s[...]
    gbuf[0:SUBLANES, :] = carry_s[j]
    gbuf[SUBLANES:SUBLANES + tile, :] = _dot(h2, wg_ref[...])
    val = _dot(h2, wv_ref[...])
    gc = cb_ref[...] + cw_ref[0:1, :] * gbuf[pl.ds(SUBLANES - (CONV_FF - 1), tile), :]
    for q in range(1, CONV_FF):
        gc = gc + cw_ref[q:q + 1, :] * gbuf[pl.ds(SUBLANES - (CONV_FF - 1) + q, tile), :]
    carry_s[j] = gbuf[tile:tile + SUBLANES, :]
    mid = (gc * jax.nn.sigmoid(gc) * val).astype(BF16)
    acc_s[...] += _dot(mid, wd_ref[...])

    @pl.when(j == n_j - 1)
    def _():
        x2 = x1_s[...] + mod_ref[0, :, 5 * d:6 * d] * acc_s[...]
        if final_norm:
            x2 = x2 * lax.rsqrt(jnp.mean(x2 * x2, axis=-1, keepdims=True) + EPS) * gf_ref[...]
        o_ref[0] = x2


def _out_ffn(x, y_a, y_b, y_c, mod, wo_a, wo_b, wo_c, g_ffn, w_up, conv_w, conv_b, w_down, g_final,
             tile, n_j, final_norm):
    bsz, seq, d = x.shape
    d_ff = w_down.shape[0]
    fc = d_ff // n_j
    c3 = lambda b, t, j: (0, 0)
    tok = lambda w: pl.BlockSpec((1, tile, w), lambda b, t, j: (b, t, 0))
    return pl.pallas_call(
        functools.partial(_out_ffn_kernel, d=d, tile=tile, n_j=n_j, final_norm=final_norm),
        grid=(bsz, seq // tile, n_j),
        in_specs=[
            tok(d), tok(y_a.shape[-1]), tok(y_b.shape[-1]), tok(y_c.shape[-1]),
            pl.BlockSpec((1, 1, mod.shape[-1]), lambda b, t, j: (b, 0, 0)),
            pl.BlockSpec(wo_a.shape, c3), pl.BlockSpec(wo_b.shape, c3), pl.BlockSpec(wo_c.shape, c3),
            pl.BlockSpec((1, d), c3),
            pl.BlockSpec((d, fc), lambda b, t, j: (0, j)),
            pl.BlockSpec((d, fc), lambda b, t, j: (0, n_j + j)),
            pl.BlockSpec((CONV_FF, fc), lambda b, t, j: (0, j)),
            pl.BlockSpec((1, fc), lambda b, t, j: (0, j)),
            pl.BlockSpec((fc, d), lambda b, t, j: (j, 0)),
            pl.BlockSpec((1, d), c3),
        ],
        out_specs=pl.BlockSpec((1, tile, d), lambda b, t, j: (b, t, 0)),
        out_shape=jax.ShapeDtypeStruct((bsz, seq, d), F32),
        scratch_shapes=[
            pltpu.VMEM((tile, d), F32),
            pltpu.VMEM((tile, d), BF16),
            pltpu.VMEM((tile, d), F32),
            pltpu.VMEM((tile + SUBLANES, fc), F32),
            pltpu.VMEM((n_j, SUBLANES, fc), F32),
        ],
        compiler_params=_params(("arbitrary", "arbitrary", "arbitrary")),
        name="out_ffn",
    )(x, y_a, y_b, y_c, mod, wo_a, wo_b, wo_c, g_ffn, w_up, w_up, conv_w, conv_b, w_down, g_final)


def _block_diag(w):
    h, n, _ = w.shape
    eye = jnp.eye(h, dtype=w.dtype)
    return (eye[:, None, :, None] * w[:, :, None, :]).reshape(h * n, h * n)


def _pick_tile(seq, want):
    tile = min(seq, want)
    while seq % tile:
        tile //= 2
    return tile


def kernel(x, c, w_mod, b_mod, norm_mix, w_in, w_out, sgu_ln_g, sgu_ln_b, sgu_w, sgu_b, lru_conv_w, lru_conv_b, lru_w_a, lru_b_a, lru_w_x, lru_b_x, lru_lambda, rwkv_mu, rwkv_w0, rwkv_w2, rwkv_a0, rwkv_a2, rwkv_g2, rwkv_k_k, rwkv_k_a, rwkv_r_k, rwkv_ln_w, rwkv_ln_b, norm_ffn, ffn_w_up, ffn_conv_w, ffn_conv_b, ffn_w_down, norm_final):
    bsz, seq, d = x.shape
    depth = w_in.shape[0]
    d_a = sgu_ln_g.shape[-1]
    d_b = lru_conv_b.shape[-1]
    d_c = rwkv_w0.shape[-1]
    p_a, p_b = 2 * d_a, 2 * d_b
    n_heads_c = d_c // HEAD_DIM

    t_proj = _pick_tile(seq, 512)
    t_sgu = _pick_tile(seq, 512)
    t_lru = _pick_tile(seq, 512)
    t_chunk = _pick_tile(seq, 256)
    t_scan = _pick_tile(seq, 128)
    t_ffn = _pick_tile(seq, 512)
    n_j = 2

    mod = _modulation(c, w_mod, b_mod)
    ones_pair = _block_diag(jnp.ones((LANES // HEAD_DIM, HEAD_DIM, HEAD_DIM), BF16))
    ones2 = jnp.concatenate([ones_pair, ones_pair], axis=0)
    tril = jnp.tril(jnp.ones((SGU_CHUNK, SGU_CHUNK), dtype=bool))
    row2 = lambda v: v.reshape(1, -1)

    for l in range(depth):
        mod_l = mod[l].reshape(bsz, 1, N_MOD * d)
        w_in_l = w_in[l].astype(BF16)
        pa, pb, pc = _in_proj(x, mod_l, row2(norm_mix[l]), w_in_l[:, :p_a], w_in_l[:, p_a:p_a + p_b],
                              w_in_l[:, p_a + p_b:], t_proj)

        sgu_wm = jnp.where(tril, sgu_w[l], 0.0).astype(BF16)
        sgu_bias = jnp.repeat(sgu_b[l].T, HEAD_DIM, axis=1)
        y_a = _sgu(pa, row2(sgu_ln_g[l]), row2(sgu_ln_b[l]), sgu_wm, sgu_bias, t_sgu)

        w_bd = jnp.concatenate([_block_diag(lru_w_a[l]), _block_diag(lru_w_x[l])], axis=1).astype(BF16)
        y_b = _rglru(pb, lru_conv_w[l], row2(lru_conv_b[l]), w_bd, row2(lru_b_a[l]), row2(lru_b_x[l]),
                     row2(lru_lambda[l]), t_lru)

        zeros_l = jnp.zeros((LORA_W, d_c), F32)
        w_lora = jnp.concatenate([
            jnp.concatenate([rwkv_w2[l], zeros_l], axis=1),
            jnp.concatenate([zeros_l, rwkv_a2[l]], axis=1)], axis=0).astype(BF16)
        rbar, mmat, obar, gbar, gate, bonus = _rwkv_chunk(
            pc, row2(rwkv_mu[l]), row2(rwkv_w0[l]), row2(rwkv_a0[l]), row2(rwkv_k_k[l]), row2(rwkv_k_a[l]),
            row2(rwkv_r_k[l]), w_lora, rwkv_g2[l].astype(BF16), ones2, t_chunk)
        y_c = _rwkv_scan(rbar, mmat, obar, gbar, gate, bonus, row2(rwkv_ln_w[l]), row2(rwkv_ln_b[l]),
                         ones2, t_scan)

        w_out_l = w_out[l].astype(BF16)
        x = _out_ffn(x, y_a, y_b, y_c, mod_l, w_out_l[:d_a], w_out_l[d_a:d_a + d_b], w_out_l[d_a + d_b:],
                     row2(norm_ffn[l]), ffn_w_up[l].astype(BF16), ffn_conv_w[l], row2(ffn_conv_b[l]),
                     ffn_w_down[l].astype(BF16), row2(norm_final), t_ffn, n_j, l == depth - 1)
    return x
```

```python
import functools

import jax
import jax.numpy as jnp
from jax import lax
from jax.experimental import pallas as pl
from jax.experimental.pallas import tpu as pltpu

HEAD_DIM = 64
SGU_CHUNK = 128
CONV_B = 4
LRU_C = 8.0
LORA_W = 64
LORA_A = 64
LORA_G = 128
CONV_FF = 3
N_MOD = 6
EPS = 1e-6
LN_EPS = 1e-5
GN_EPS = 64e-5

RWKV_CHUNK = 64
LANES = 128
MXU_TILE = 256
OUT_FFN_TRACE_ORDER = "hfhfff"
SUBLANES = 8
VMEM_LIMIT = 56 * 1024 * 1024
F32_MIN_NORMAL = 1.1754944e-38

BF16 = jnp.bfloat16
F32 = jnp.float32


def _dot(a, b):
    return jnp.dot(a, b, preferred_element_type=F32)


def _dot_nt(a, b):
    return lax.dot_general(a, b, (((1,), (1,)), ((), ())), preferred_element_type=F32)


def _dot_tn(a, b):
    return lax.dot_general(a, b, (((0,), (0,)), ((), ())), preferred_element_type=F32)


def _split(x):
    hi = x.astype(BF16)
    lo = (x - hi.astype(F32)).astype(BF16)
    return hi, lo


def _head_sums(x, ones2):
    outs = []
    for pi in range(x.shape[-1] // LANES):
        hi, lo = _split(x[:, pi * LANES:(pi + 1) * LANES])
        outs.append(_dot(jnp.concatenate([hi, lo], axis=1), ones2))
    return jnp.concatenate(outs, axis=1)


def _softplus(z):
    return jnp.maximum(z, 0.0) + jnp.log1p(jnp.exp(-jnp.abs(z)))


def _params(sem):
    return pltpu.CompilerParams(dimension_semantics=sem, vmem_limit_bytes=VMEM_LIMIT)


def _mod_kernel(c_ref, w_ref, b_ref, o_ref):
    c = c_ref[...]
    ca = c * jax.nn.sigmoid(c)
    w = w_ref[0]
    chi, clo = _split(ca)
    whi, wlo = _split(w)
    acc = _dot(chi, whi) + _dot(chi, wlo) + _dot(clo, whi)
    o_ref[0] = acc + b_ref[0]


def _modulation(c, w_mod, b_mod):
    depth, d, nd = w_mod.shape
    bsz = c.shape[0]
    nblk = nd // d
    return pl.pallas_call(
        _mod_kernel,
        grid=(depth, nblk),
        in_specs=[
            pl.BlockSpec((bsz, d), lambda l, j: (0, 0)),
            pl.BlockSpec((1, d, d), lambda l, j: (l, 0, j)),
            pl.BlockSpec((1, 1, d), lambda l, j: (l, 0, j)),
        ],
        out_specs=pl.BlockSpec((1, bsz, d), lambda l, j: (l, 0, j)),
        out_shape=jax.ShapeDtypeStruct((depth, bsz, nd), F32),
        compiler_params=_params(("arbitrary", "arbitrary")),
        name="adaln_mod",
    )(c, w_mod, b_mod.reshape(depth, 1, nd))


def _in_proj_kernel(x_ref, mod_ref, g_ref, wa_ref, wb_ref, wc_ref, pa_ref, pb_ref, pc_ref, *, d):
    x = x_ref[0]
    y = x * lax.rsqrt(jnp.mean(x * x, axis=-1, keepdims=True) + EPS)
    h = (y * g_ref[...]) * (1.0 + mod_ref[0, :, d:2 * d]) + mod_ref[0, :, 0:d]
    hb = h.astype(BF16)
    pa_ref[0] = _dot(hb, wa_ref[...])
    pb_ref[0] = _dot(hb, wb_ref[...])
    pc_ref[0] = _dot(hb, wc_ref[...])


def _in_proj(x, mod, g, w_a, w_b, w_c, tile):
    bsz, seq, d = x.shape
    na, nb, nc = w_a.shape[1], w_b.shape[1], w_c.shape[1]
    full = lambda b, t: (0, 0)
    return pl.pallas_call(
        functools.partial(_in_proj_kernel, d=d),
        grid=(bsz, seq // tile),
        in_specs=[
            pl.BlockSpec((1, tile, d), lambda b, t: (b, t, 0)),
            pl.BlockSpec((1, 1, mod.shape[-1]), lambda b, t: (b, 0, 0)),
            pl.BlockSpec((1, d), full),
            pl.BlockSpec((d, na), full),
            pl.BlockSpec((d, nb), full),
            pl.BlockSpec((d, nc), full),
        ],
        out_specs=[
            pl.BlockSpec((1, tile, na), lambda b, t: (b, t, 0)),
            pl.BlockSpec((1, tile, nb), lambda b, t: (b, t, 0)),
            pl.BlockSpec((1, tile, nc), lambda b, t: (b, t, 0)),
        ],
        out_shape=[
            jax.ShapeDtypeStruct((bsz, seq, na), F32),
            jax.ShapeDtypeStruct((bsz, seq, nb), F32),
            jax.ShapeDtypeStruct((bsz, seq, nc), F32),
        ],
        compiler_params=_params(("arbitrary", "arbitrary")),
        name="in_proj",
    )(x, mod, g, w_a, w_b, w_c)


def _sgu_kernel(p_ref, lng_ref, lnb_ref, w_ref, bias_ref, o_ref, *, d_a, n_heads, n_chunks):
    z = jax.nn.gelu(p_ref[0])
    u = z[:, :d_a]
    v = z[:, d_a:]
    mu = jnp.mean(v, axis=-1, keepdims=True)
    var = jnp.mean(jnp.square(v - mu), axis=-1, keepdims=True)
    vn = ((v - mu) * lax.rsqrt(var + LN_EPS) * lng_ref[...] + lnb_ref[...]).astype(BF16)
    rows = []
    for c in range(n_chunks):
        vc = vn[c * SGU_CHUNK:(c + 1) * SGU_CHUNK]
        cols = [_dot(w_ref[h], vc[:, h * HEAD_DIM:(h + 1) * HEAD_DIM]) for h in range(n_heads)]
        rows.append(jnp.concatenate(cols, axis=1) + bias_ref[...])
    mixed = jnp.concatenate(rows, axis=0) if n_chunks > 1 else rows[0]
    o_ref[0] = (u * mixed).astype(o_ref.dtype)


def _sgu(p_a, ln_g, ln_b, w_masked, bias_cols, tile):
    bsz, seq, two_da = p_a.shape
    d_a = two_da // 2
    n_heads = d_a // HEAD_DIM
    full2 = lambda b, t: (0, 0)
    return pl.pallas_call(
        functools.partial(_sgu_kernel, d_a=d_a, n_heads=n_heads, n_chunks=tile // SGU_CHUNK),
        grid=(bsz, seq // tile),
        in_specs=[
            pl.BlockSpec((1, tile, two_da), lambda b, t: (b, t, 0)),
            pl.BlockSpec((1, d_a), full2),
            pl.BlockSpec((1, d_a), full2),
            pl.BlockSpec((n_heads, SGU_CHUNK, SGU_CHUNK), lambda b, t: (0, 0, 0)),
            pl.BlockSpec((SGU_CHUNK, d_a), full2),
        ],
        out_specs=pl.BlockSpec((1, tile, d_a), lambda b, t: (b, t, 0)),
        out_shape=jax.ShapeDtypeStruct((bsz, seq, d_a), BF16),
        compiler_params=_params(("arbitrary", "arbitrary")),
        name="sgu",
    )(p_a, ln_g, ln_b, w_masked, bias_cols)


def _rglru_kernel(p_ref, cw_ref, cb_ref, wbd_ref, bra_ref, bix_ref, lam_ref, o_ref,
                  xtail, hprev, *, d_b, tile):
    t = pl.program_id(1)

    @pl.when(t == 0)
    def _():
        xtail[...] = jnp.zeros_like(xtail)
        hprev[...] = jnp.zeros_like(hprev)

    n_grp = tile // SUBLANES
    sub = lax.broadcasted_iota(jnp.int32, (n_grp, SUBLANES, d_b), 1)

    x_new = p_ref[0, :, 0:d_b]
    yg = p_ref[0, :, d_b:2 * d_b]
    x_ext = jnp.concatenate([xtail[...], x_new], axis=0).reshape(n_grp + 1, SUBLANES, d_b)
    xtail[...] = x_new[tile - SUBLANES:tile, :]
    xr3 = cb_ref[...] + cw_ref[CONV_B - 1:CONV_B, :] * x_ext[1:]
    for k in range(1, CONV_B):
        rot = pltpu.roll(x_ext, k, 1)
        xr3 = xr3 + cw_ref[CONV_B - 1 - k:CONV_B - k, :] * jnp.where(sub >= k, rot[1:], rot[:-1])
    xr = xr3.reshape(tile, d_b)

    ri = _dot(xr.astype(BF16), wbd_ref[...])
    r = jax.nn.sigmoid(ri[:, 0:d_b] + bra_ref[...])
    i = jax.nn.sigmoid(ri[:, d_b:2 * d_b] + bix_ref[...])
    log_a = (-LRU_C * r) * _softplus(-lam_ref[...])
    a = jnp.exp(log_a)
    z = 2.0 * log_a
    u = a * a
    log_u = jnp.log(u)
    one_minus_a2 = jnp.where(log_u == 0.0, -z, (1.0 - u) * (z / log_u))
    bterm = (one_minus_a2 * lax.rsqrt(jnp.maximum(one_minus_a2, F32_MIN_NORMAL))) * (i * xr)

    a3 = a.reshape(n_grp, SUBLANES, d_b)
    b3 = bterm.reshape(n_grp, SUBLANES, d_b)
    shift = 1
    while shift < SUBLANES:
        keep = sub >= shift
        a_s = jnp.where(keep, pltpu.roll(a3, shift, 1), 1.0)
        b_s = jnp.where(keep, pltpu.roll(b3, shift, 1), 0.0)
        b3 = a3 * b_s + b3
        a3 = a3 * a_s
        shift *= 2
    h = hprev[...]
    groups = []
    for g in range(n_grp):
        hg = b3[g] + a3[g] * h
        groups.append(hg)
        h = hg[SUBLANES - 1:SUBLANES, :]
    hprev[...] = h
    o_ref[0] = (jax.nn.gelu(yg) * jnp.concatenate(groups, axis=0)).astype(o_ref.dtype)


def _rglru(p_b, conv_w, conv_b, w_bd, b_ra, b_ix, lam, tile):
    bsz, seq, two_db = p_b.shape
    d_b = two_db // 2
    full2 = lambda b, t: (0, 0)
    return pl.pallas_call(
        functools.partial(_rglru_kernel, d_b=d_b, tile=tile),
        grid=(bsz, seq // tile),
        in_specs=[
            pl.BlockSpec((1, tile, two_db), lambda b, t: (b, t, 0)),
            pl.BlockSpec((CONV_B, d_b), full2),
            pl.BlockSpec((1, d_b), full2),
            pl.BlockSpec((d_b, 2 * d_b), full2),
            pl.BlockSpec((1, d_b), full2),
            pl.BlockSpec((1, d_b), full2),
            pl.BlockSpec((1, d_b), full2),
        ],
        out_specs=pl.BlockSpec((1, tile, d_b), lambda b, t: (b, t, 0)),
        out_shape=jax.ShapeDtypeStruct((bsz, seq, d_b), BF16),
        scratch_shapes=[
            pltpu.VMEM((SUBLANES, d_b), F32),
            pltpu.VMEM((1, d_b), F32),
        ],
        compiler_params=_params(("arbitrary", "arbitrary")),
        name="rglru",
    )(p_b, conv_w, conv_b, w_bd, b_ra, b_ix, lam)


def _stack_pair(x, lane_lo):
    return jnp.concatenate([jnp.where(lane_lo, x, 0.0), jnp.where(lane_lo, 0.0, x)], axis=0)


def _unstack_pair(x):
    return x[0:RWKV_CHUNK] + x[RWKV_CHUNK:2 * RWKV_CHUNK]


RWKV_STAGED = ("kt", "kh", "bh", "khl", "bhl", "v", "rt", "gl", "gate", "bonus")
RWKV_STAGED_BF16 = 6
RWKV_TRACE_ORDER = "cpccpccpccp"


def _rwkv_prep(p_ref, mu_ref, w0_ref, a0_ref, kk_ref, ka_ref, rk_ref, wl_ref, g2_ref, ones_ref, xbuf, dst,
               *, d_c, tile):
    n_chunks = tile // RWKV_CHUNK
    xbuf[SUBLANES:SUBLANES + tile, :] = p_ref[0]
    p = p_ref[0]
    prev = xbuf[pl.ds(SUBLANES - 1, tile), :]
    xbuf[0:SUBLANES, :] = xbuf[tile:tile + SUBLANES, :]
    ps = p + (prev - p) * mu_ref[...]

    r = ps[:, 0:d_c]
    k = ps[:, d_c:2 * d_c]
    v = ps[:, 2 * d_c:3 * d_c]
    xwa = ps[:, 3 * d_c:3 * d_c + LORA_W + LORA_A]
    xg = ps[:, 3 * d_c + LORA_W + LORA_A:]

    lane_l = lax.broadcasted_iota(jnp.int32, xwa.shape, 1)
    lin = jnp.where(lane_l < LORA_W, jnp.tanh(xwa), xwa).astype(BF16)
    wa_lora = _dot(lin, wl_ref[...])
    w = -_softplus(-(w0_ref[...] + wa_lora[:, 0:d_c])) - 0.5
    lw = -jnp.exp(w)
    a = jax.nn.sigmoid(a0_ref[...] + wa_lora[:, d_c:2 * d_c])
    dst["gate"][...] = _dot(jax.nn.sigmoid(xg).astype(BF16), g2_ref[...])

    yield
    ones2 = ones_ref[...]
    kk = k * kk_ref[...]
    kappa = kk * lax.rsqrt(jnp.maximum(_head_sums(kk * kk, ones2), 1e-24))
    kmod = k * (1.0 + (a - 1.0) * ka_ref[...])
    bvec = kappa * a
    yield
    dst["bonus"][...] = _head_sums(r * kmod * rk_ref[...], ones2) * v

    ri = lax.broadcasted_iota(jnp.int32, (LANES, 2 * LANES), 0)
    ci = lax.broadcasted_iota(jnp.int32, (LANES, 2 * LANES), 1) & (LANES - 1)
    shift = RWKV_CHUNK.bit_length() - 1
    tril2 = jnp.where((jnp.right_shift(ri, shift) == jnp.right_shift(ci, shift)) & (ci <= ri),
                      1.0, 0.0).astype(BF16)
    lw_hi, lw_lo = _split(lw)
    c_parts = []
    for m in range(tile // LANES):
        rws = slice(m * LANES, (m + 1) * LANES)
        c_parts.append(_dot(tril2, jnp.concatenate([lw_hi[rws], lw_lo[rws]], axis=0)))
    c_incl = jnp.concatenate(c_parts, axis=0) if len(c_parts) > 1 else c_parts[0]
    c_tot = jnp.concatenate(
        [jnp.broadcast_to(c_incl[(c + 1) * RWKV_CHUNK - 1:(c + 1) * RWKV_CHUNK, :], (RWKV_CHUNK, d_c))
         for c in range(n_chunks)], axis=0)

    yield
    e_neg = jnp.exp(-c_incl)
    e_end = jnp.exp(c_tot - c_incl)
    dst["kt"][...] = (kappa * jnp.exp(c_incl - lw)).astype(BF16)
    dst["rt"][...] = r * jnp.exp(c_incl)
    dst["kh"][...] = (kmod * e_neg).astype(BF16)
    dst["bh"][...] = (bvec * e_neg).astype(BF16)
    dst["khl"][...] = (kmod * e_end).astype(BF16)
    dst["bhl"][...] = (bvec * e_end).astype(BF16)
    dst["v"][...] = v.astype(BF16)
    dst["gl"][...] = jnp.exp(c_tot)


def _rwkv_chains(src, rbar_ref, mmat_ref, obar_ref, gbar_ref, gate_ref, bonus_ref, *, d_c, tile):
    n_chunks = tile // RWKV_CHUNK
    n_pairs = d_c // LANES
    gate_ref[0] = src["gate"][...]
    bonus_ref[0] = src["bonus"][...]

    ii = lax.broadcasted_iota(jnp.int32, (LANES, LANES), 0)
    jj = lax.broadcasted_iota(jnp.int32, (LANES, LANES), 1)
    strict = ii > jj
    incl = ii >= jj
    eye = ii == jj
    lane_lo = lax.broadcasted_iota(jnp.int32, (RWKV_CHUNK, LANES), 1) < HEAD_DIM

    chains = [(c, pi) for c in range(n_chunks) for pi in range(n_pairs)]

    def rows_of(ch):
        return slice(ch[0] * RWKV_CHUNK, (ch[0] + 1) * RWKV_CHUNK)

    def lanes_of(ch):
        return slice(ch[1] * LANES, (ch[1] + 1) * LANES)

    def stacked(name, ch):
        return _stack_pair(src[name][rows_of(ch), lanes_of(ch)], lane_lo)

    rts = [stacked("rt", ch) for ch in chains]
    kts_b = [stacked("kt", ch) for ch in chains]
    vbd = [stacked("v", ch) for ch in chains]

    sc = [_dot_nt(jnp.concatenate([kts_b[n], rts[n].astype(BF16)], axis=0),
                  jnp.concatenate([stacked("kh", ch), stacked("bh", ch)], axis=0))
          for n, ch in enumerate(chains)]
    a_k = [jnp.where(strict, s[0:LANES, 0:LANES], 0.0).astype(BF16) for s in sc]
    n_b = [jnp.where(strict, s[0:LANES, LANES:2 * LANES], 0.0) for s in sc]
    p_m = [jnp.where(incl, s[LANES:2 * LANES, 0:LANES], 0.0).astype(BF16) for s in sc]
    q_m = [jnp.where(incl, s[LANES:2 * LANES, LANES:2 * LANES], 0.0).astype(BF16) for s in sc]
    yield

    x_m = [jnp.where(eye, 1.0, 0.0) - n for n in n_b]
    pw = [_dot(n.astype(BF16), n.astype(BF16)) for n in n_b]
    yield
    for _ in range(4):
        y = [_dot(jnp.concatenate([x.astype(BF16), p.astype(BF16)], axis=0), p.astype(BF16))
             for x, p in zip(x_m, pw)]
        x_m = [x + yy[0:LANES] for x, yy in zip(x_m, y)]
        pw = [yy[LANES:2 * LANES] for yy in y]
        yield
    t_m = [(x + _dot(x.astype(BF16), p.astype(BF16))).astype(BF16) for x, p in zip(x_m, pw)]
    yield

    akv = [_dot(a, vv) for a, vv in zip(a_k, vbd)]
    yield
    ty_b = [_dot(tm, jnp.concatenate([kb, a.astype(BF16)], axis=1)).astype(BF16)
            for tm, kb, a in zip(t_m, kts_b, akv)]
    yield
    rhs = [jnp.concatenate([jnp.concatenate([jnp.zeros_like(vv), vv], axis=1), tyb], axis=0)
           for vv, tyb in zip(vbd, ty_b)]
    tok = [_dot(jnp.concatenate([p, -q], axis=1), rr) for p, q, rr in zip(p_m, q_m, rhs)]
    chn = [_dot_tn(jnp.concatenate([stacked("khl", ch), -stacked("bhl", ch)], axis=0), rr)
           for ch, rr in zip(chains, rhs)]
    yield

    for n, ch in enumerate(chains):
        rows, lanes = rows_of(ch), lanes_of(ch)
        rbar = stacked("rt", ch) + tok[n][:, 0:LANES]
        gl_row = src["gl"][ch[0] * RWKV_CHUNK:ch[0] * RWKV_CHUNK + 1, lanes]
        mmat = jnp.where(eye, gl_row, 0.0) + chn[n][:, 0:LANES]
        rbar_ref[0, rows, lanes] = _unstack_pair(rbar).astype(rbar_ref.dtype)
        mmat_ref[0, rows, lanes] = _unstack_pair(mmat).astype(mmat_ref.dtype)
        obar_ref[0, rows, lanes] = _unstack_pair(tok[n][:, LANES:2 * LANES])
        gbar_ref[0, rows, lanes] = _unstack_pair(chn[n][:, LANES:2 * LANES])


def _rwkv_chunk_kernel(p_ref, mu_ref, w0_ref, a0_ref, kk_ref, ka_ref, rk_ref, wl_ref, g2_ref, ones_ref,
                       rbar_ref, mmat_ref, obar_ref, gbar_ref, gate_ref, bonus_ref,
                       xbuf, *staged, d_c, tile, tiles_per_seq):
    s = pl.program_id(0)
    n = len(RWKV_STAGED)
    set_a = dict(zip(RWKV_STAGED, staged[:n]))
    set_b = dict(zip(RWKV_STAGED, staged[n:]))

    @pl.when(s == 0)
    def _():
        for ref in staged[n:]:
            ref[...] = jnp.zeros_like(ref)

    @pl.when(lax.rem(s, tiles_per_seq) == 0)
    def _():
        xbuf[0:SUBLANES, :] = jnp.zeros((SUBLANES, xbuf.shape[-1]), F32)

    def step(dst, src):
        prep = _rwkv_prep(p_ref, mu_ref, w0_ref, a0_ref, kk_ref, ka_ref, rk_ref, wl_ref, g2_ref, ones_ref, xbuf,
                          dst, d_c=d_c, tile=tile)
        chains = _rwkv_chains(src, rbar_ref, mmat_ref, obar_ref, gbar_ref, gate_ref, bonus_ref,
                              d_c=d_c, tile=tile)
        for who in RWKV_TRACE_ORDER:
            next(prep if who == "p" else chains, None)
        for _ in chains:
            pass
        for _ in prep:
            pass

    parity = lax.rem(s, 2)

    @pl.when(parity == 0)
    def _():
        step(set_a, set_b)

    @pl.when(parity == 1)
    def _():
        step(set_b, set_a)


def _rwkv_chunk(p_c, mu, w0, a0, k_k, k_a, r_k, w_lora, g2, ones2, tile):
    bsz, seq, pc = p_c.shape
    d_c = w0.shape[-1]
    n_t = seq // tile
    n_tiles = bsz * n_t
    full2 = lambda s: (0, 0)

    def staged_tile(s):
        q = jnp.minimum(s, n_tiles - 1)
        return (q // n_t, q % n_t, 0)

    def finished_tile(s):
        q = jnp.maximum(s - 1, 0)
        return (q // n_t, q % n_t, 0)

    tok = pl.BlockSpec((1, tile, d_c), finished_tile)
    shp = lambda dt: jax.ShapeDtypeStruct((bsz, seq, d_c), dt)
    vec = pl.BlockSpec((1, d_c), full2)
    one_set = ([pltpu.VMEM((tile, d_c), BF16)] * RWKV_STAGED_BF16
               + [pltpu.VMEM((tile, d_c), F32)] * (len(RWKV_STAGED) - RWKV_STAGED_BF16))
    return pl.pallas_call(
        functools.partial(_rwkv_chunk_kernel, d_c=d_c, tile=tile, tiles_per_seq=n_t),
        grid=(n_tiles + 1,),
        in_specs=[
            pl.BlockSpec((1, tile, pc), staged_tile),
            pl.BlockSpec((1, pc), full2),
            vec, vec, vec, vec, vec,
            pl.BlockSpec(w_lora.shape, full2),
            pl.BlockSpec(g2.shape, full2),
            pl.BlockSpec(ones2.shape, full2),
        ],
        out_specs=[tok, tok, tok, tok, tok, tok],
        out_shape=[shp(BF16), shp(BF16), shp(F32), shp(F32), shp(F32), shp(F32)],
        scratch_shapes=[pltpu.VMEM((tile + SUBLANES, pc), F32)] + one_set + one_set,
        compiler_params=_params(("arbitrary",)),
        name="rwkv_chunk",
    )(p_c, mu, w0, a0, k_k, k_a, r_k, w_lora, g2, ones2)


def _rwkv_scan_kernel(rbar_ref, mmat_ref, obar_ref, gbar_ref, gate_ref, bonus_ref, lnw_ref, lnb_ref, ones_ref,
                      y_ref, h_s, o_s, *, d_c, tile):
    t = pl.program_id(0)
    bsz = obar_ref.shape[0]
    n_chunks = tile // RWKV_CHUNK
    n_pairs = d_c // LANES

    @pl.when(t == 0)
    def _():
        h_s[...] = jnp.zeros_like(h_s)

    lane_lo = lax.broadcasted_iota(jnp.int32, (RWKV_CHUNK, LANES), 1) < HEAD_DIM
    chains = [(b, pi) for b in range(bsz) for pi in range(n_pairs)]

    for c in range(n_chunks):
        rows = slice(c * RWKV_CHUNK, (c + 1) * RWKV_CHUNK)
        outs = []
        for n, (b, pi) in enumerate(chains):
            lanes = slice(pi * LANES, (pi + 1) * LANES)
            hbd = _stack_pair(h_s[n], lane_lo).astype(BF16)
            lhs = jnp.concatenate([rbar_ref[b, rows, lanes], mmat_ref[b, rows, lanes]], axis=0)
            outs.append(_dot(lhs, hbd))
        for n, (b, pi) in enumerate(chains):
            lanes = slice(pi * LANES, (pi + 1) * LANES)
            o_s[b, rows, lanes] = outs[n][0:RWKV_CHUNK] + obar_ref[b, rows, lanes]
            h_s[n] = outs[n][RWKV_CHUNK:2 * RWKV_CHUNK] + gbar_ref[b, rows, lanes]

    ones2 = ones_ref[...]
    inv = 1.0 / HEAD_DIM
    for b in range(bsz):
        o = o_s[b]
        mean = _head_sums(o, ones2) * inv
        dlt = o - mean
        var = _head_sums(dlt * dlt, ones2) * inv
        gn = dlt * lax.rsqrt(var + GN_EPS) * lnw_ref[...] + lnb_ref[...]
        y_ref[b] = ((gn + bonus_ref[b]) * gate_ref[b]).astype(y_ref.dtype)


def _rwkv_scan(rbar, mmat, obar, gbar, gate, bonus, ln_w, ln_b, ones2, tile):
    bsz, seq, d_c = obar.shape
    full2 = lambda t: (0, 0)
    tok = pl.BlockSpec((bsz, tile, d_c), lambda t: (0, t, 0))
    vec = pl.BlockSpec((1, d_c), full2)
    return pl.pallas_call(
        functools.partial(_rwkv_scan_kernel, d_c=d_c, tile=tile),
        grid=(seq // tile,),
        in_specs=[tok, tok, tok, tok, tok, tok, vec, vec, pl.BlockSpec(ones2.shape, full2)],
        out_specs=tok,
        out_shape=jax.ShapeDtypeStruct((bsz, seq, d_c), BF16),
        scratch_shapes=[
            pltpu.VMEM((bsz * (d_c // LANES), RWKV_CHUNK, LANES), F32),
            pltpu.VMEM((bsz, tile, d_c), F32),
        ],
        compiler_params=_params(("arbitrary",)),
        name="rwkv_scan",
    )(rbar, mmat, obar, gbar, gate, bonus, ln_w, ln_b, ones2)


def _mixer_out(x_ref, ya_ref, yb_ref, yc_ref, mod_ref, wo_ref, g_ref, dst, *, d):
    y = _dot(jnp.concatenate([ya_ref[0], yb_ref[0], yc_ref[0]], axis=1), wo_ref[...])
    yield
    x1 = x_ref[0] + mod_ref[0, :, 2 * d:3 * d] * y
    dst["x1"][...] = x1
    n = x1 * lax.rsqrt(jnp.mean(x1 * x1, axis=-1, keepdims=True) + EPS)
    h2 = (n * g_ref[...]) * (1.0 + mod_ref[0, :, 4 * d:5 * d]) + mod_ref[0, :, 3 * d:4 * d]
    dst["h2"][...] = h2.astype(BF16)


def _conv_glu_ffn(src, mod_ref, wup_ref, cw_ref, cb_ref, wd_ref, gf_ref, o_ref, gbuf, gtail,
                  *, d, tile, d_ff, splits, final_norm):
    h2 = src["h2"][...]
    mids = []
    for lo, hi in splits:
        gbuf[0:SUBLANES, lo:hi] = gtail[:, lo:hi]
        gbuf[SUBLANES:SUBLANES + tile, lo:hi] = _dot(h2, wup_ref[:, lo:hi])
        yield
        val = _dot(h2, wup_ref[:, d_ff + lo:d_ff + hi])
        gc = cb_ref[:, lo:hi] + cw_ref[0:1, lo:hi] * gbuf[pl.ds(SUBLANES - (CONV_FF - 1), tile), lo:hi]
        for q in range(1, CONV_FF):
            gc = gc + cw_ref[q:q + 1, lo:hi] * gbuf[pl.ds(SUBLANES - (CONV_FF - 1) + q, tile), lo:hi]
        gtail[:, lo:hi] = gbuf[tile:tile + SUBLANES, lo:hi]
        mids.append((gc * jax.nn.sigmoid(gc) * val).astype(BF16))
        yield
    ffn = _dot(mids[0], wd_ref[splits[0][0]:splits[0][1], :])
    for mid, (lo, hi) in zip(mids[1:], splits[1:]):
        ffn = ffn + _dot(mid, wd_ref[lo:hi, :])
    x2 = src["x1"][...] + mod_ref[0, :, 5 * d:6 * d] * ffn
    if final_norm:
        x2 = x2 * lax.rsqrt(jnp.mean(x2 * x2, axis=-1, keepdims=True) + EPS) * gf_ref[...]
    o_ref[0] = x2


def _out_ffn_kernel(x_ref, ya_ref, yb_ref, yc_ref, mod_in_ref, mod_out_ref, wo_ref, g_ref,
                    wup_ref, cw_ref, cb_ref, wd_ref, gf_ref, o_ref,
                    gbuf, gtail, x1_a, h2_a, x1_b, h2_b, *, d, tile, d_ff, splits, tiles_per_seq, final_norm):
    s = pl.program_id(0)
    set_a = {"x1": x1_a, "h2": h2_a}
    set_b = {"x1": x1_b, "h2": h2_b}

    @pl.when(s == 0)
    def _():
        x1_b[...] = jnp.zeros_like(x1_b)
        h2_b[...] = jnp.zeros_like(h2_b)

    @pl.when(lax.rem(jnp.maximum(s - 1, 0), tiles_per_seq) == 0)
    def _():
        gtail[...] = jnp.zeros_like(gtail)

    def step(dst, src):
        head = _mixer_out(x_ref, ya_ref, yb_ref, yc_ref, mod_in_ref, wo_ref, g_ref, dst, d=d)
        ffn = _conv_glu_ffn(src, mod_out_ref, wup_ref, cw_ref, cb_ref, wd_ref, gf_ref, o_ref, gbuf, gtail,
                            d=d, tile=tile, d_ff=d_ff, splits=splits, final_norm=final_norm)
        for who in OUT_FFN_TRACE_ORDER:
            next(head if who == "h" else ffn, None)
        for _ in ffn:
            pass
        for _ in head:
            pass

    parity = lax.rem(s, 2)

    @pl.when(parity == 0)
    def _():
        step(set_a, set_b)

    @pl.when(parity == 1)
    def _():
        step(set_b, set_a)


def _column_splits(width):
    half = (width // (2 * MXU_TILE)) * MXU_TILE or width // 2
    return ((0, width - half), (width - half, width))


def _out_ffn(x, y_a, y_b, y_c, mod, w_out, g_ffn, w_up, conv_w, conv_b, w_down, g_final, tile, final_norm):
    bsz, seq, d = x.shape
    d_ff = w_down.shape[0]
    n_t = seq // tile
    n_tiles = bsz * n_t
    const = lambda s: (0, 0)
    resident = lambda a: pl.BlockSpec(a.shape, const, pipeline_mode=pl.Buffered(1))

    def staged(s):
        q = jnp.minimum(s, n_tiles - 1)
        return q // n_t, q % n_t

    def finished(s):
        q = jnp.maximum(s - 1, 0)
        return q // n_t, q % n_t

    tok_in = lambda w: pl.BlockSpec((1, tile, w), lambda s: (*staged(s), 0))
    mod_spec = lambda which: pl.BlockSpec((1, 1, mod.shape[-1]), lambda s: (which(s)[0], 0, 0))
    return pl.pallas_call(
        functools.partial(_out_ffn_kernel, d=d, tile=tile, d_ff=d_ff, splits=_column_splits(d_ff),
                          tiles_per_seq=n_t, final_norm=final_norm),
        grid=(n_tiles + 1,),
        in_specs=[
            tok_in(d), tok_in(y_a.shape[-1]), tok_in(y_b.shape[-1]), tok_in(y_c.shape[-1]),
            mod_spec(staged), mod_spec(finished),
            resident(w_out),
            pl.BlockSpec((1, d), const),
            resident(w_up),
            pl.BlockSpec(conv_w.shape, const),
            pl.BlockSpec(conv_b.shape, const),
            resident(w_down),
            pl.BlockSpec((1, d), const),
        ],
        out_specs=pl.BlockSpec((1, tile, d), lambda s: (*finished(s), 0)),
        out_shape=jax.ShapeDtypeStruct((bsz, seq, d), F32),
        scratch_shapes=[
            pltpu.VMEM((tile + SUBLANES, d_ff), F32),
            pltpu.VMEM((SUBLANES, d_ff), F32),
            pltpu.VMEM((tile, d), F32), pltpu.VMEM((tile, d), BF16),
            pltpu.VMEM((tile, d), F32), pltpu.VMEM((tile, d), BF16),
        ],
        compiler_params=_params(("arbitrary",)),
        name="out_ffn",
    )(x, y_a, y_b, y_c, mod, mod, w_out, g_ffn, w_up, conv_w, conv_b, w_down, g_final)


def _block_diag(w):
    h, n, _ = w.shape
    eye = jnp.eye(h, dtype=w.dtype)
    return (eye[:, None, :, None] * w[:, :, None, :]).reshape(h * n, h * n)


def _pick_tile(seq, want):
    tile = min(seq, want)
    while seq % tile:
        tile //= 2
    return tile


def kernel(x, c, w_mod, b_mod, norm_mix, w_in, w_out, sgu_ln_g, sgu_ln_b, sgu_w, sgu_b, lru_conv_w, lru_conv_b, lru_w_a, lru_b_a, lru_w_x, lru_b_x, lru_lambda, rwkv_mu, rwkv_w0, rwkv_w2, rwkv_a0, rwkv_a2, rwkv_g2, rwkv_k_k, rwkv_k_a, rwkv_r_k, rwkv_ln_w, rwkv_ln_b, norm_ffn, ffn_w_up, ffn_conv_w, ffn_conv_b, ffn_w_down, norm_final):
    bsz, seq, d = x.shape
    depth = w_in.shape[0]
    d_a = sgu_ln_g.shape[-1]
    d_b = lru_conv_b.shape[-1]
    d_c = rwkv_w0.shape[-1]
    p_a, p_b = 2 * d_a, 2 * d_b

    t_proj = _pick_tile(seq, 512)
    t_sgu = _pick_tile(seq, 512)
    t_lru = _pick_tile(seq, 512)
    t_chunk = _pick_tile(seq, 256)
    t_scan = _pick_tile(seq, 128)
    t_ffn = _pick_tile(seq, 512)

    mod = _modulation(c, w_mod, b_mod)
    ones_pair = _block_diag(jnp.ones((LANES // HEAD_DIM, HEAD_DIM, HEAD_DIM), BF16))
    ones2 = jnp.concatenate([ones_pair, ones_pair], axis=0)
    tril = jnp.tril(jnp.ones((SGU_CHUNK, SGU_CHUNK), dtype=bool))
    row2 = lambda v: v.reshape(1, -1)

    for l in range(depth):
        mod_l = mod[l].reshape(bsz, 1, N_MOD * d)
        w_in_l = w_in[l].astype(BF16)
        pa, pb, pc = _in_proj(x, mod_l, row2(norm_mix[l]), w_in_l[:, :p_a], w_in_l[:, p_a:p_a + p_b],
                              w_in_l[:, p_a + p_b:], t_proj)

        sgu_wm = jnp.where(tril, sgu_w[l], 0.0).astype(BF16)
        sgu_bias = jnp.repeat(sgu_b[l].T, HEAD_DIM, axis=1)
        y_a = _sgu(pa, row2(sgu_ln_g[l]), row2(sgu_ln_b[l]), sgu_wm, sgu_bias, t_sgu)

        w_bd = jnp.concatenate([_block_diag(lru_w_a[l]), _block_diag(lru_w_x[l])], axis=1).astype(BF16)
        y_b = _rglru(pb, lru_conv_w[l], row2(lru_conv_b[l]), w_bd, row2(lru_b_a[l]), row2(lru_b_x[l]),
                     row2(lru_lambda[l]), t_lru)

        zeros_l = jnp.zeros((LORA_W, d_c), F32)
        w_lora = jnp.concatenate([
            jnp.concatenate([rwkv_w2[l], zeros_l], axis=1),
            jnp.concatenate([zeros_l, rwkv_a2[l]], axis=1)], axis=0).astype(BF16)
        rbar, mmat, obar, gbar, gate, bonus = _rwkv_chunk(
            pc, row2(rwkv_mu[l]), row2(rwkv_w0[l]), row2(rwkv_a0[l]), row2(rwkv_k_k[l]), row2(rwkv_k_a[l]),
            row2(rwkv_r_k[l]), w_lora, rwkv_g2[l].astype(BF16), ones2, t_chunk)
        y_c = _rwkv_scan(rbar, mmat, obar, gbar, gate, bonus, row2(rwkv_ln_w[l]), row2(rwkv_ln_b[l]),
                         ones2, t_scan)

        w_out_l = w_out[l].astype(BF16)
        x = _out_ffn(x, y_a, y_b, y_c, mod_l, w_out_l, row2(norm_ffn[l]), ffn_w_up[l].astype(BF16),
                     ffn_conv_w[l], row2(ffn_conv_b[l]), ffn_w_down[l].astype(BF16), row2(norm_final),
                     t_ffn, l == depth - 1)
    return x
```

```python
import functools

import jax
import jax.numpy as jnp
from jax import lax
from jax.experimental import pallas as pl
from jax.experimental.pallas import tpu as pltpu

HEAD_DIM = 64
SGU_CHUNK = 128
CONV_B = 4
LRU_C = 8.0
LORA_W = 64
LORA_A = 64
LORA_G = 128
CONV_FF = 3
N_MOD = 6
EPS = 1e-6
LN_EPS = 1e-5
GN_EPS = 64e-5

RWKV_CHUNK = 64
LANES = 128
MXU_TILE = 256
IN_PROJ_ROW_BLOCK = 128
OUT_FFN_TRACE_ORDER = "hfhfff"
SUBLANES = 8
VMEM_LIMIT = 56 * 1024 * 1024
F32_MIN_NORMAL = 1.1754944e-38

BF16 = jnp.bfloat16
F32 = jnp.float32


def _dot(a, b):
    return jnp.dot(a, b, preferred_element_type=F32)


def _dot_nt(a, b):
    return lax.dot_general(a, b, (((1,), (1,)), ((), ())), preferred_element_type=F32)


def _dot_tn(a, b):
    return lax.dot_general(a, b, (((0,), (0,)), ((), ())), preferred_element_type=F32)


def _split(x):
    hi = x.astype(BF16)
    lo = (x - hi.astype(F32)).astype(BF16)
    return hi, lo


def _head_sums(x, ones2):
    outs = []
    for pi in range(x.shape[-1] // LANES):
        hi, lo = _split(x[:, pi * LANES:(pi + 1) * LANES])
        outs.append(_dot(jnp.concatenate([hi, lo], axis=1), ones2))
    return jnp.concatenate(outs, axis=1)


def _softplus(z):
    return jnp.maximum(z, 0.0) + jnp.log1p(jnp.exp(-jnp.abs(z)))


def _params(sem):
    return pltpu.CompilerParams(dimension_semantics=sem, vmem_limit_bytes=VMEM_LIMIT)


def _layer_slab(arr, layer, single_buffer=False):
    zeros = (0,) * (arr.ndim - 1)
    mode = {"pipeline_mode": pl.Buffered(1)} if single_buffer else {}
    return pl.BlockSpec((None,) + arr.shape[1:], lambda *_: (layer,) + zeros, **mode)


def _mod_rows(mod, layer, batch_of):
    return pl.BlockSpec((None, 1, 1, mod.shape[-1]), lambda *g: (layer, batch_of(*g), 0, 0))


def _mod_kernel(c_ref, w_ref, b_ref, o_ref):
    c = c_ref[...]
    ca = c * jax.nn.sigmoid(c)
    w = w_ref[0]
    chi, clo = _split(ca)
    whi, wlo = _split(w)
    acc = _dot(chi, whi) + _dot(chi, wlo) + _dot(clo, whi)
    o_ref[0] = acc + b_ref[0]


def _modulation(c, w_mod, b_mod):
    depth, d, nd = w_mod.shape
    bsz = c.shape[0]
    nblk = nd // d
    return pl.pallas_call(
        _mod_kernel,
        grid=(depth, nblk),
        in_specs=[
            pl.BlockSpec((bsz, d), lambda l, j: (0, 0)),
            pl.BlockSpec((1, d, d), lambda l, j: (l, 0, j)),
            pl.BlockSpec((1, 1, d), lambda l, j: (l, 0, j)),
        ],
        out_specs=pl.BlockSpec((1, bsz, d), lambda l, j: (l, 0, j)),
        out_shape=jax.ShapeDtypeStruct((depth, bsz, nd), F32),
        compiler_params=_params(("arbitrary", "arbitrary")),
        name="adaln_mod",
    )(c, w_mod, b_mod.reshape(depth, 1, nd))


def _in_proj_kernel(x_ref, mod_ref, g_ref, w_ref, pa_ref, pb_ref, pc_ref, *, d, tile, row_block):
    na, nb = pa_ref.shape[-1], pb_ref.shape[-1]
    scale = 1.0 + mod_ref[0, :, d:2 * d]
    shift = mod_ref[0, :, 0:d]
    for r in range(tile // row_block):
        rows = slice(r * row_block, (r + 1) * row_block)
        x = x_ref[0, rows, :]
        y = x * lax.rsqrt(jnp.mean(x * x, axis=-1, keepdims=True) + EPS)
        h = (y * g_ref[...]) * scale + shift
        p = _dot(h.astype(BF16), w_ref[...])
        pa_ref[0, rows, :] = p[:, 0:na]
        pb_ref[0, rows, :] = p[:, na:na + nb]
        pc_ref[0, rows, :] = p[:, na + nb:]


def _in_proj(x, mod, g, w_in, widths, tile, layer):
    bsz, seq, d = x.shape
    na, nb, nc = widths
    return pl.pallas_call(
        functools.partial(_in_proj_kernel, d=d, tile=tile, row_block=min(tile, IN_PROJ_ROW_BLOCK)),
        grid=(bsz, seq // tile),
        in_specs=[
            pl.BlockSpec((1, tile, d), lambda b, t: (b, t, 0)),
            _mod_rows(mod, layer, lambda b, t: b),
            _layer_slab(g, layer),
            _layer_slab(w_in, layer, single_buffer=True),
        ],
        out_specs=[
            pl.BlockSpec((1, tile, na), lambda b, t: (b, t, 0)),
            pl.BlockSpec((1, tile, nb), lambda b, t: (b, t, 0)),
            pl.BlockSpec((1, tile, nc), lambda b, t: (b, t, 0)),
        ],
        out_shape=[
            jax.ShapeDtypeStruct((bsz, seq, na), F32),
            jax.ShapeDtypeStruct((bsz, seq, nb), F32),
            jax.ShapeDtypeStruct((bsz, seq, nc), F32),
        ],
        compiler_params=_params(("arbitrary", "arbitrary")),
        name="in_proj",
    )(x, mod, g, w_in)


def _sgu_kernel(p_ref, lng_ref, lnb_ref, w_ref, bias_ref, o_ref, *, d_a, n_heads, n_chunks):
    z = jax.nn.gelu(p_ref[0])
    u = z[:, :d_a]
    v = z[:, d_a:]
    mu = jnp.mean(v, axis=-1, keepdims=True)
    var = jnp.mean(jnp.square(v - mu), axis=-1, keepdims=True)
    vn = ((v - mu) * lax.rsqrt(var + LN_EPS) * lng_ref[...] + lnb_ref[...]).astype(BF16)
    lane_lo = lax.broadcasted_iota(jnp.int32, (SGU_CHUNK, LANES), 1) < HEAD_DIM
    rows = []
    for c in range(n_chunks):
        cols = []
        for q in range(d_a // LANES):
            vq = vn[c * SGU_CHUNK:(c + 1) * SGU_CHUNK, q * LANES:(q + 1) * LANES]
            cols.append(_dot(w_ref[q], _stack_pair(vq, lane_lo)))
        rows.append(jnp.concatenate(cols, axis=1) + bias_ref[...])
    mixed = jnp.concatenate(rows, axis=0) if n_chunks > 1 else rows[0]
    o_ref[0] = (u * mixed).astype(o_ref.dtype)


def _sgu(p_a, ln_g, ln_b, w_masked, bias_cols, tile, layer):
    bsz, seq, two_da = p_a.shape
    d_a = two_da // 2
    n_heads = d_a // HEAD_DIM
    return pl.pallas_call(
        functools.partial(_sgu_kernel, d_a=d_a, n_heads=n_heads, n_chunks=tile // SGU_CHUNK),
        grid=(bsz, seq // tile),
        in_specs=[
            pl.BlockSpec((1, tile, two_da), lambda b, t: (b, t, 0)),
            _layer_slab(ln_g, layer), _layer_slab(ln_b, layer),
            _layer_slab(w_masked, layer), _layer_slab(bias_cols, layer),
        ],
        out_specs=pl.BlockSpec((1, tile, d_a), lambda b, t: (b, t, 0)),
        out_shape=jax.ShapeDtypeStruct((bsz, seq, d_a), BF16),
        compiler_params=_params(("arbitrary", "arbitrary")),
        name="sgu",
    )(p_a, ln_g, ln_b, w_masked, bias_cols)


def _rglru_kernel(p_ref, cw_ref, cb_ref, wbd_ref, bra_ref, bix_ref, lam_ref, o_ref,
                  xtail, hprev, *, d_b, tile):
    t = pl.program_id(1)

    @pl.when(t == 0)
    def _():
        xtail[...] = jnp.zeros_like(xtail)
        hprev[...] = jnp.zeros_like(hprev)

    n_grp = tile // SUBLANES
    sub = lax.broadcasted_iota(jnp.int32, (n_grp, SUBLANES, d_b), 1)

    x_new = p_ref[0, :, 0:d_b]
    yg = p_ref[0, :, d_b:2 * d_b]
    x_ext = jnp.concatenate([xtail[...], x_new], axis=0).reshape(n_grp + 1, SUBLANES, d_b)
    xtail[...] = x_new[tile - SUBLANES:tile, :]
    xr3 = cb_ref[...] + cw_ref[CONV_B - 1:CONV_B, :] * x_ext[1:]
    for k in range(1, CONV_B):
        rot = pltpu.roll(x_ext, k, 1)
        xr3 = xr3 + cw_ref[CONV_B - 1 - k:CONV_B - k, :] * jnp.where(sub >= k, rot[1:], rot[:-1])
    xr = xr3.reshape(tile, d_b)

    ri = _dot(xr.astype(BF16), wbd_ref[...])
    r = jax.nn.sigmoid(ri[:, 0:d_b] + bra_ref[...])
    i = jax.nn.sigmoid(ri[:, d_b:2 * d_b] + bix_ref[...])
    log_a = (-LRU_C * r) * _softplus(-lam_ref[...])
    a = jnp.exp(log_a)
    z = 2.0 * log_a
    u = a * a
    log_u = jnp.log(u)
    one_minus_a2 = jnp.where(log_u == 0.0, -z, (1.0 - u) * (z / log_u))
    bterm = (one_minus_a2 * lax.rsqrt(jnp.maximum(one_minus_a2, F32_MIN_NORMAL))) * (i * xr)

    a3 = a.reshape(n_grp, SUBLANES, d_b)
    b3 = bterm.reshape(n_grp, SUBLANES, d_b)
    shift = 1
    while shift < SUBLANES:
        keep = sub >= shift
        a_s = jnp.where(keep, pltpu.roll(a3, shift, 1), 1.0)
        b_s = jnp.where(keep, pltpu.roll(b3, shift, 1), 0.0)
        b3 = a3 * b_s + b3
        a3 = a3 * a_s
        shift *= 2
    h = hprev[...]
    groups = []
    for g in range(n_grp):
        hg = b3[g] + a3[g] * h
        groups.append(hg)
        h = hg[SUBLANES - 1:SUBLANES, :]
    hprev[...] = h
    o_ref[0] = (jax.nn.gelu(yg) * jnp.concatenate(groups, axis=0)).astype(o_ref.dtype)


def _rglru(p_b, conv_w, conv_b, w_bd, b_ra, b_ix, lam, tile, layer):
    bsz, seq, two_db = p_b.shape
    d_b = two_db // 2
    return pl.pallas_call(
        functools.partial(_rglru_kernel, d_b=d_b, tile=tile),
        grid=(bsz, seq // tile),
        in_specs=[pl.BlockSpec((1, tile, two_db), lambda b, t: (b, t, 0))]
        + [_layer_slab(a, layer) for a in (conv_w, conv_b, w_bd, b_ra, b_ix, lam)],
        out_specs=pl.BlockSpec((1, tile, d_b), lambda b, t: (b, t, 0)),
        out_shape=jax.ShapeDtypeStruct((bsz, seq, d_b), BF16),
        scratch_shapes=[
            pltpu.VMEM((SUBLANES, d_b), F32),
            pltpu.VMEM((1, d_b), F32),
        ],
        compiler_params=_params(("arbitrary", "arbitrary")),
        name="rglru",
    )(p_b, conv_w, conv_b, w_bd, b_ra, b_ix, lam)


def _stack_pair(x, lane_lo):
    return jnp.concatenate([jnp.where(lane_lo, x, 0.0), jnp.where(lane_lo, 0.0, x)], axis=0)


def _unstack_pair(x):
    return x[0:RWKV_CHUNK] + x[RWKV_CHUNK:2 * RWKV_CHUNK]


RWKV_STAGED = ("kt", "kh", "bh", "khl", "bhl", "v", "rt", "gl", "gate", "bonus")
RWKV_STAGED_BF16 = 6
RWKV_TRACE_ORDER = "cpccpccpccp"


def _rwkv_prep(p_ref, mu_ref, w0_ref, a0_ref, kk_ref, ka_ref, rk_ref, wl_ref, g2_ref, ones_ref, xbuf, dst,
               *, d_c, tile):
    n_chunks = tile // RWKV_CHUNK
    xbuf[SUBLANES:SUBLANES + tile, :] = p_ref[0]
    p = p_ref[0]
    prev = xbuf[pl.ds(SUBLANES - 1, tile), :]
    xbuf[0:SUBLANES, :] = xbuf[tile:tile + SUBLANES, :]
    ps = p + (prev - p) * mu_ref[...]

    r = ps[:, 0:d_c]
    k = ps[:, d_c:2 * d_c]
    v = ps[:, 2 * d_c:3 * d_c]
    xwa = ps[:, 3 * d_c:3 * d_c + LORA_W + LORA_A]
    xg = ps[:, 3 * d_c + LORA_W + LORA_A:]

    lane_l = lax.broadcasted_iota(jnp.int32, xwa.shape, 1)
    lin = jnp.where(lane_l < LORA_W, jnp.tanh(xwa), xwa).astype(BF16)
    wa_lora = _dot(lin, wl_ref[...])
    w = -_softplus(-(w0_ref[...] + wa_lora[:, 0:d_c])) - 0.5
    lw = -jnp.exp(w)
    a = jax.nn.sigmoid(a0_ref[...] + wa_lora[:, d_c:2 * d_c])
    dst["gate"][...] = _dot(jax.nn.sigmoid(xg).astype(BF16), g2_ref[...])

    yield
    ones2 = ones_ref[...]
    kk = k * kk_ref[...]
    kappa = kk * lax.rsqrt(jnp.maximum(_head_sums(kk * kk, ones2), 1e-24))
    kmod = k * (1.0 + (a - 1.0) * ka_ref[...])
    bvec = kappa * a
    yield
    dst["bonus"][...] = _head_sums(r * kmod * rk_ref[...], ones2) * v

    ri = lax.broadcasted_iota(jnp.int32, (LANES, 2 * LANES), 0)
    ci = lax.broadcasted_iota(jnp.int32, (LANES, 2 * LANES), 1) & (LANES - 1)
    shift = RWKV_CHUNK.bit_length() - 1
    tril2 = jnp.where((jnp.right_shift(ri, shift) == jnp.right_shift(ci, shift)) & (ci <= ri),
                      1.0, 0.0).astype(BF16)
    lw_hi, lw_lo = _split(lw)
    c_parts = []
    for m in range(tile // LANES):
        rws = slice(m * LANES, (m + 1) * LANES)
        c_parts.append(_dot(tril2, jnp.concatenate([lw_hi[rws], lw_lo[rws]], axis=0)))
    c_incl = jnp.concatenate(c_parts, axis=0) if len(c_parts) > 1 else c_parts[0]
    c_tot = jnp.concatenate(
        [jnp.broadcast_to(c_incl[(c + 1) * RWKV_CHUNK - 1:(c + 1) * RWKV_CHUNK, :], (RWKV_CHUNK, d_c))
         for c in range(n_chunks)], axis=0)

    yield
    e_neg = jnp.exp(-c_incl)
    e_end = jnp.exp(c_tot - c_incl)
    dst["kt"][...] = (kappa * jnp.exp(c_incl - lw)).astype(BF16)
    dst["rt"][...] = r * jnp.exp(c_incl)
    dst["kh"][...] = (kmod * e_neg).astype(BF16)
    dst["bh"][...] = (bvec * e_neg).astype(BF16)
    dst["khl"][...] = (kmod * e_end).astype(BF16)
    dst["bhl"][...] = (bvec * e_end).astype(BF16)
    dst["v"][...] = v.astype(BF16)
    dst["gl"][...] = jnp.exp(c_tot)


def _rwkv_chains(src, rbar_ref, mmat_ref, obar_ref, gbar_ref, gate_ref, bonus_ref, *, d_c, tile):
    n_chunks = tile // RWKV_CHUNK
    n_pairs = d_c // LANES
    gate_ref[0] = src["gate"][...]
    bonus_ref[0] = src["bonus"][...]

    ii = lax.broadcasted_iota(jnp.int32, (LANES, LANES), 0)
    jj = lax.broadcasted_iota(jnp.int32, (LANES, LANES), 1)
    strict = ii > jj
    incl = ii >= jj
    eye = ii == jj
    lane_lo = lax.broadcasted_iota(jnp.int32, (RWKV_CHUNK, LANES), 1) < HEAD_DIM

    chains = [(c, pi) for c in range(n_chunks) for pi in range(n_pairs)]

    def rows_of(ch):
        return slice(ch[0] * RWKV_CHUNK, (ch[0] + 1) * RWKV_CHUNK)

    def lanes_of(ch):
        return slice(ch[1] * LANES, (ch[1] + 1) * LANES)

    def stacked(name, ch):
        return _stack_pair(src[name][rows_of(ch), lanes_of(ch)], lane_lo)

    rts = [stacked("rt", ch) for ch in chains]
    kts_b = [stacked("kt", ch) for ch in chains]
    vbd = [stacked("v", ch) for ch in chains]

    sc = [_dot_nt(jnp.concatenate([kts_b[n], rts[n].astype(BF16)], axis=0),
                  jnp.concatenate([stacked("kh", ch), stacked("bh", ch)], axis=0))
          for n, ch in enumerate(chains)]
    a_k = [jnp.where(strict, s[0:LANES, 0:LANES], 0.0).astype(BF16) for s in sc]
    n_b = [jnp.where(strict, s[0:LANES, LANES:2 * LANES], 0.0) for s in sc]
    p_m = [jnp.where(incl, s[LANES:2 * LANES, 0:LANES], 0.0).astype(BF16) for s in sc]
    q_m = [jnp.where(incl, s[LANES:2 * LANES, LANES:2 * LANES], 0.0).astype(BF16) for s in sc]
    yield

    x_m = [jnp.where(eye, 1.0, 0.0) - n for n in n_b]
    pw = [_dot(n.astype(BF16), n.astype(BF16)) for n in n_b]
    yield
    for _ in range(4):
        y = [_dot(jnp.concatenate([x.astype(BF16), p.astype(BF16)], axis=0), p.astype(BF16))
             for x, p in zip(x_m, pw)]
        x_m = [x + yy[0:LANES] for x, yy in zip(x_m, y)]
        pw = [yy[LANES:2 * LANES] for yy in y]
        yield
    t_m = [(x + _dot(x.astype(BF16), p.astype(BF16))).astype(BF16) for x, p in zip(x_m, pw)]
    yield

    akv = [_dot(a, vv) for a, vv in zip(a_k, vbd)]
    yield
    ty_b = [_dot(tm, jnp.concatenate([kb, a.astype(BF16)], axis=1)).astype(BF16)
            for tm, kb, a in zip(t_m, kts_b, akv)]
    yield
    rhs = [jnp.concatenate([jnp.concatenate([jnp.zeros_like(vv), vv], axis=1), tyb], axis=0)
           for vv, tyb in zip(vbd, ty_b)]
    tok = [_dot(jnp.concatenate([p, -q], axis=1), rr) for p, q, rr in zip(p_m, q_m, rhs)]
    chn = [_dot_tn(jnp.concatenate([stacked("khl", ch), -stacked("bhl", ch)], axis=0), rr)
           for ch, rr in zip(chains, rhs)]
    yield

    for n, ch in enumerate(chains):
        rows, lanes = rows_of(ch), lanes_of(ch)
        rbar = stacked("rt", ch) + tok[n][:, 0:LANES]
        gl_row = src["gl"][ch[0] * RWKV_CHUNK:ch[0] * RWKV_CHUNK + 1, lanes]
        mmat = jnp.where(eye, gl_row, 0.0) + chn[n][:, 0:LANES]
        rbar_ref[0, rows, lanes] = _unstack_pair(rbar).astype(rbar_ref.dtype)
        mmat_ref[0, rows, lanes] = _unstack_pair(mmat).astype(mmat_ref.dtype)
        obar_ref[0, rows, lanes] = _unstack_pair(tok[n][:, LANES:2 * LANES])
        gbar_ref[0, rows, lanes] = _unstack_pair(chn[n][:, LANES:2 * LANES])


def _rwkv_chunk_kernel(p_ref, mu_ref, w0_ref, a0_ref, kk_ref, ka_ref, rk_ref, wl_ref, g2_ref, ones_ref,
                       rbar_ref, mmat_ref, obar_ref, gbar_ref, gate_ref, bonus_ref,
                       xbuf, *staged, d_c, tile, tiles_per_seq):
    s = pl.program_id(0)
    n = len(RWKV_STAGED)
    set_a = dict(zip(RWKV_STAGED, staged[:n]))
    set_b = dict(zip(RWKV_STAGED, staged[n:]))

    @pl.when(s == 0)
    def _():
        for ref in staged[n:]:
            ref[...] = jnp.zeros_like(ref)

    @pl.when(lax.rem(s, tiles_per_seq) == 0)
    def _():
        xbuf[0:SUBLANES, :] = jnp.zeros((SUBLANES, xbuf.shape[-1]), F32)

    def step(dst, src):
        prep = _rwkv_prep(p_ref, mu_ref, w0_ref, a0_ref, kk_ref, ka_ref, rk_ref, wl_ref, g2_ref, ones_ref, xbuf,
                          dst, d_c=d_c, tile=tile)
        chains = _rwkv_chains(src, rbar_ref, mmat_ref, obar_ref, gbar_ref, gate_ref, bonus_ref,
                              d_c=d_c, tile=tile)
        for who in RWKV_TRACE_ORDER:
            next(prep if who == "p" else chains, None)
        for _ in chains:
            pass
        for _ in prep:
            pass

    parity = lax.rem(s, 2)

    @pl.when(parity == 0)
    def _():
        step(set_a, set_b)

    @pl.when(parity == 1)
    def _():
        step(set_b, set_a)


def _rwkv_chunk(p_c, mu, w0, a0, k_k, k_a, r_k, w_lora, g2, ones2, tile, layer):
    bsz, seq, pc = p_c.shape
    d_c = w0.shape[-1]
    n_t = seq // tile
    n_tiles = bsz * n_t

    def staged_tile(s):
        q = jnp.minimum(s, n_tiles - 1)
        return (q // n_t, q % n_t, 0)

    def finished_tile(s):
        q = jnp.maximum(s - 1, 0)
        return (q // n_t, q % n_t, 0)

    tok = pl.BlockSpec((1, tile, d_c), finished_tile)
    shp = lambda dt: jax.ShapeDtypeStruct((bsz, seq, d_c), dt)
    one_set = ([pltpu.VMEM((tile, d_c), BF16)] * RWKV_STAGED_BF16
               + [pltpu.VMEM((tile, d_c), F32)] * (len(RWKV_STAGED) - RWKV_STAGED_BF16))
    return pl.pallas_call(
        functools.partial(_rwkv_chunk_kernel, d_c=d_c, tile=tile, tiles_per_seq=n_t),
        grid=(n_tiles + 1,),
        in_specs=[pl.BlockSpec((1, tile, pc), staged_tile)]
        + [_layer_slab(a, layer) for a in (mu, w0, a0, k_k, k_a, r_k, w_lora, g2)]
        + [pl.BlockSpec(ones2.shape, lambda s: (0, 0))],
        out_specs=[tok, tok, tok, tok, tok, tok],
        out_shape=[shp(BF16), shp(BF16), shp(F32), shp(F32), shp(F32), shp(F32)],
        scratch_shapes=[pltpu.VMEM((tile + SUBLANES, pc), F32)] + one_set + one_set,
        compiler_params=_params(("arbitrary",)),
        name="rwkv_chunk",
    )(p_c, mu, w0, a0, k_k, k_a, r_k, w_lora, g2, ones2)


def _rwkv_scan_kernel(rbar_ref, mmat_ref, obar_ref, gbar_ref, gate_ref, bonus_ref, lnw_ref, lnb_ref, ones_ref,
                      y_ref, h_s, o_s, *, d_c, tile):
    t = pl.program_id(0)
    bsz = obar_ref.shape[0]
    n_chunks = tile // RWKV_CHUNK
    n_pairs = d_c // LANES

    @pl.when(t == 0)
    def _():
        h_s[...] = jnp.zeros_like(h_s)

    lane_lo = lax.broadcasted_iota(jnp.int32, (RWKV_CHUNK, LANES), 1) < HEAD_DIM
    chains = [(b, pi) for b in range(bsz) for pi in range(n_pairs)]

    ones2 = ones_ref[...]
    inv = 1.0 / HEAD_DIM

    def finish(c):
        rows = slice(c * RWKV_CHUNK, (c + 1) * RWKV_CHUNK)
        o = o_s[:, rows, :].reshape(bsz * RWKV_CHUNK, d_c)
        mean = _head_sums(o, ones2) * inv
        dlt = o - mean
        var = _head_sums(dlt * dlt, ones2) * inv
        gn = (dlt * lax.rsqrt(var + GN_EPS) * lnw_ref[...] + lnb_ref[...]).reshape(bsz, RWKV_CHUNK, d_c)
        y_ref[:, rows, :] = ((gn + bonus_ref[:, rows, :]) * gate_ref[:, rows, :]).astype(y_ref.dtype)

    for c in range(n_chunks):
        rows = slice(c * RWKV_CHUNK, (c + 1) * RWKV_CHUNK)
        outs = []
        for n, (b, pi) in enumerate(chains):
            lanes = slice(pi * LANES, (pi + 1) * LANES)
            hbd = _stack_pair(h_s[n], lane_lo).astype(BF16)
            lhs = jnp.concatenate([rbar_ref[b, rows, lanes], mmat_ref[b, rows, lanes]], axis=0)
            outs.append(_dot(lhs, hbd))
        if c > 0:
            finish(c - 1)
        for n, (b, pi) in enumerate(chains):
            lanes = slice(pi * LANES, (pi + 1) * LANES)
            o_s[b, rows, lanes] = outs[n][0:RWKV_CHUNK] + obar_ref[b, rows, lanes]
            h_s[n] = outs[n][RWKV_CHUNK:2 * RWKV_CHUNK] + gbar_ref[b, rows, lanes]
    finish(n_chunks - 1)


def _rwkv_scan(rbar, mmat, obar, gbar, gate, bonus, ln_w, ln_b, ones2, tile, layer):
    bsz, seq, d_c = obar.shape
    tok = pl.BlockSpec((bsz, tile, d_c), lambda t: (0, t, 0))
    return pl.pallas_call(
        functools.partial(_rwkv_scan_kernel, d_c=d_c, tile=tile),
        grid=(seq // tile,),
        in_specs=[tok, tok, tok, tok, tok, tok, _layer_slab(ln_w, layer), _layer_slab(ln_b, layer),
                  pl.BlockSpec(ones2.shape, lambda t: (0, 0))],
        out_specs=tok,
        out_shape=jax.ShapeDtypeStruct((bsz, seq, d_c), BF16),
        scratch_shapes=[
            pltpu.VMEM((bsz * (d_c // LANES), RWKV_CHUNK, LANES), F32),
            pltpu.VMEM((bsz, tile, d_c), F32),
        ],
        compiler_params=_params(("arbitrary",)),
        name="rwkv_scan",
    )(rbar, mmat, obar, gbar, gate, bonus, ln_w, ln_b, ones2)


def _mixer_out(x_ref, ya_ref, yb_ref, yc_ref, mod_ref, wo_ref, g_ref, dst, *, d):
    y = _dot(jnp.concatenate([ya_ref[0], yb_ref[0], yc_ref[0]], axis=1), wo_ref[...])
    yield
    x1 = x_ref[0] + mod_ref[0, :, 2 * d:3 * d] * y
    dst["x1"][...] = x1
    n = x1 * lax.rsqrt(jnp.mean(x1 * x1, axis=-1, keepdims=True) + EPS)
    h2 = (n * g_ref[...]) * (1.0 + mod_ref[0, :, 4 * d:5 * d]) + mod_ref[0, :, 3 * d:4 * d]
    dst["h2"][...] = h2.astype(BF16)


def _conv_glu_ffn(src, mod_ref, wup_ref, cw_ref, cb_ref, wd_ref, gf_ref, o_ref, gbuf, gtail,
                  *, d, tile, d_ff, splits, final_norm):
    h2 = src["h2"][...]
    mids = []
    for lo, hi in splits:
        gbuf[0:SUBLANES, lo:hi] = gtail[:, lo:hi]
        gbuf[SUBLANES:SUBLANES + tile, lo:hi] = _dot(h2, wup_ref[:, lo:hi])
        yield
        val = _dot(h2, wup_ref[:, d_ff + lo:d_ff + hi])
        gc = cb_ref[:, lo:hi] + cw_ref[0:1, lo:hi] * gbuf[pl.ds(SUBLANES - (CONV_FF - 1), tile), lo:hi]
        for q in range(1, CONV_FF):
            gc = gc + cw_ref[q:q + 1, lo:hi] * gbuf[pl.ds(SUBLANES - (CONV_FF - 1) + q, tile), lo:hi]
        gtail[:, lo:hi] = gbuf[tile:tile + SUBLANES, lo:hi]
        mids.append((gc * jax.nn.sigmoid(gc) * val).astype(BF16))
        yield
    ffn = _dot(mids[0], wd_ref[splits[0][0]:splits[0][1], :])
    for mid, (lo, hi) in zip(mids[1:], splits[1:]):
        ffn = ffn + _dot(mid, wd_ref[lo:hi, :])
    x2 = src["x1"][...] + mod_ref[0, :, 5 * d:6 * d] * ffn
    if final_norm:
        x2 = x2 * lax.rsqrt(jnp.mean(x2 * x2, axis=-1, keepdims=True) + EPS) * gf_ref[...]
    o_ref[0] = x2


def _out_ffn_kernel(x_ref, ya_ref, yb_ref, yc_ref, mod_in_ref, mod_out_ref, wo_ref, g_ref,
                    wup_ref, cw_ref, cb_ref, wd_ref, gf_ref, o_ref,
                    gbuf, gtail, x1_a, h2_a, x1_b, h2_b, *, d, tile, d_ff, splits, tiles_per_seq, final_norm):
    s = pl.program_id(0)
    set_a = {"x1": x1_a, "h2": h2_a}
    set_b = {"x1": x1_b, "h2": h2_b}

    @pl.when(s == 0)
    def _():
        x1_b[...] = jnp.zeros_like(x1_b)
        h2_b[...] = jnp.zeros_like(h2_b)

    @pl.when(lax.rem(jnp.maximum(s - 1, 0), tiles_per_seq) == 0)
    def _():
        gtail[...] = jnp.zeros_like(gtail)

    def step(dst, src):
        head = _mixer_out(x_ref, ya_ref, yb_ref, yc_ref, mod_in_ref, wo_ref, g_ref, dst, d=d)
        ffn = _conv_glu_ffn(src, mod_out_ref, wup_ref, cw_ref, cb_ref, wd_ref, gf_ref, o_ref, gbuf, gtail,
                            d=d, tile=tile, d_ff=d_ff, splits=splits, final_norm=final_norm)
        for who in OUT_FFN_TRACE_ORDER:
            next(head if who == "h" else ffn, None)
        for _ in ffn:
            pass
        for _ in head:
            pass

    parity = lax.rem(s, 2)

    @pl.when(parity == 0)
    def _():
        step(set_a, set_b)

    @pl.when(parity == 1)
    def _():
        step(set_b, set_a)


def _column_splits(width):
    half = (width // (2 * MXU_TILE)) * MXU_TILE or width // 2
    return ((0, width - half), (width - half, width))


def _out_ffn(x, y_a, y_b, y_c, mod, w_out, g_ffn, w_up, conv_w, conv_b, w_down, g_final, tile, final_norm,
             layer):
    bsz, seq, d = x.shape
    d_ff = w_down.shape[1]
    n_t = seq // tile
    n_tiles = bsz * n_t
    resident = lambda a: _layer_slab(a, layer, single_buffer=True)

    def staged(s):
        q = jnp.minimum(s, n_tiles - 1)
        return q // n_t, q % n_t

    def finished(s):
        q = jnp.maximum(s - 1, 0)
        return q // n_t, q % n_t

    tok_in = lambda w: pl.BlockSpec((1, tile, w), lambda s: (*staged(s), 0))
    return pl.pallas_call(
        functools.partial(_out_ffn_kernel, d=d, tile=tile, d_ff=d_ff, splits=_column_splits(d_ff),
                          tiles_per_seq=n_t, final_norm=final_norm),
        grid=(n_tiles + 1,),
        in_specs=[
            tok_in(d), tok_in(y_a.shape[-1]), tok_in(y_b.shape[-1]), tok_in(y_c.shape[-1]),
            _mod_rows(mod, layer, lambda s: staged(s)[0]), _mod_rows(mod, layer, lambda s: finished(s)[0]),
            resident(w_out),
            _layer_slab(g_ffn, layer),
            resident(w_up),
            _layer_slab(conv_w, layer),
            _layer_slab(conv_b, layer),
            resident(w_down),
            pl.BlockSpec((1, d), lambda s: (0, 0)),
        ],
        out_specs=pl.BlockSpec((1, tile, d), lambda s: (*finished(s), 0)),
        out_shape=jax.ShapeDtypeStruct((bsz, seq, d), F32),
        scratch_shapes=[
            pltpu.VMEM((tile + SUBLANES, d_ff), F32),
            pltpu.VMEM((SUBLANES, d_ff), F32),
            pltpu.VMEM((tile, d), F32), pltpu.VMEM((tile, d), BF16),
            pltpu.VMEM((tile, d), F32), pltpu.VMEM((tile, d), BF16),
        ],
        compiler_params=_params(("arbitrary",)),
        name="out_ffn",
    )(x, y_a, y_b, y_c, mod, mod, w_out, g_ffn, w_up, conv_w, conv_b, w_down, g_final)


def _block_diag(w):
    h, n, _ = w.shape
    eye = jnp.eye(h, dtype=w.dtype)
    return (eye[:, None, :, None] * w[:, :, None, :]).reshape(h * n, h * n)


def _pick_tile(seq, want):
    tile = min(seq, want)
    while seq % tile:
        tile //= 2
    return tile


def kernel(x, c, w_mod, b_mod, norm_mix, w_in, w_out, sgu_ln_g, sgu_ln_b, sgu_w, sgu_b, lru_conv_w, lru_conv_b, lru_w_a, lru_b_a, lru_w_x, lru_b_x, lru_lambda, rwkv_mu, rwkv_w0, rwkv_w2, rwkv_a0, rwkv_a2, rwkv_g2, rwkv_k_k, rwkv_k_a, rwkv_r_k, rwkv_ln_w, rwkv_ln_b, norm_ffn, ffn_w_up, ffn_conv_w, ffn_conv_b, ffn_w_down, norm_final):
    bsz, seq, d = x.shape
    depth = w_in.shape[0]
    d_a = sgu_ln_g.shape[-1]
    d_b = lru_conv_b.shape[-1]
    d_c = rwkv_w0.shape[-1]
    p_a, p_b = 2 * d_a, 2 * d_b

    t_proj = _pick_tile(seq, 512)
    t_sgu = _pick_tile(seq, 512)
    t_lru = _pick_tile(seq, 512)
    t_chunk = _pick_tile(seq, 256)
    t_scan = _pick_tile(seq, 256)
    t_ffn = _pick_tile(seq, 512)

    rows = lambda v: v.reshape(depth, 1, -1)
    mod = _modulation(c, w_mod, b_mod).reshape(depth, bsz, 1, N_MOD * d)
    ones_pair = _block_diag(jnp.ones((LANES // HEAD_DIM, HEAD_DIM, HEAD_DIM), BF16))
    ones2 = jnp.concatenate([ones_pair, ones_pair], axis=0)
    tril = jnp.tril(jnp.ones((SGU_CHUNK, SGU_CHUNK), dtype=bool))
    w_in_b = w_in.astype(BF16)
    w_out_b = w_out.astype(BF16)
    w_up_b = ffn_w_up.astype(BF16)
    w_down_b = ffn_w_down.astype(BF16)
    heads_per_group = LANES // HEAD_DIM
    sgu_wm = jnp.where(tril, sgu_w, 0.0).astype(BF16)
    sgu_wm = sgu_wm.reshape(depth, -1, heads_per_group, SGU_CHUNK, SGU_CHUNK)
    sgu_wm = jnp.swapaxes(sgu_wm, 2, 3).reshape(depth, -1, SGU_CHUNK, heads_per_group * SGU_CHUNK)
    sgu_bias = jnp.repeat(jnp.swapaxes(sgu_b, 1, 2), HEAD_DIM, axis=2)
    block_diag_l = jax.vmap(_block_diag)
    lru_w_bd = jnp.concatenate([block_diag_l(lru_w_a), block_diag_l(lru_w_x)], axis=2).astype(BF16)
    zeros_l = jnp.zeros((depth, LORA_W, d_c), F32)
    w_lora = jnp.concatenate([
        jnp.concatenate([rwkv_w2, zeros_l], axis=2),
        jnp.concatenate([zeros_l, rwkv_a2], axis=2)], axis=1).astype(BF16)
    g2_b = rwkv_g2.astype(BF16)

    for l in range(depth):
        pa, pb, pc = _in_proj(x, mod, rows(norm_mix), w_in_b, (p_a, p_b, w_in.shape[-1] - p_a - p_b), t_proj, l)
        y_a = _sgu(pa, rows(sgu_ln_g), rows(sgu_ln_b), sgu_wm, sgu_bias, t_sgu, l)
        y_b = _rglru(pb, lru_conv_w, rows(lru_conv_b), lru_w_bd, rows(lru_b_a), rows(lru_b_x),
                     rows(lru_lambda), t_lru, l)
        rbar, mmat, obar, gbar, gate, bonus = _rwkv_chunk(
            pc, rows(rwkv_mu), rows(rwkv_w0), rows(rwkv_a0), rows(rwkv_k_k), rows(rwkv_k_a), rows(rwkv_r_k),
            w_lora, g2_b, ones2, t_chunk, l)
        y_c = _rwkv_scan(rbar, mmat, obar, gbar, gate, bonus, rows(rwkv_ln_w), rows(rwkv_ln_b), ones2, t_scan, l)
        x = _out_ffn(x, y_a, y_b, y_c, mod, w_out_b, rows(norm_ffn), w_up_b, ffn_conv_w, rows(ffn_conv_b),
                     w_down_b, norm_final.reshape(1, -1), t_ffn, l == depth - 1, l)
    return x
```

```python
import functools

import jax
import jax.numpy as jnp
from jax import lax
from jax.experimental import pallas as pl
from jax.experimental.pallas import tpu as pltpu

HEAD_DIM = 64
SGU_CHUNK = 128
CONV_B = 4
LRU_C = 8.0
LORA_W = 64
LORA_A = 64
LORA_G = 128
CONV_FF = 3
N_MOD = 6
EPS = 1e-6
LN_EPS = 1e-5
GN_EPS = 64e-5

RWKV_CHUNK = 64
LANES = 128
MXU_TILE = 256
IN_PROJ_ROW_BLOCK = 128
OUT_FFN_TRACE_ORDER = "hfhfff"
SUBLANES = 8
VMEM_LIMIT = 56 * 1024 * 1024
F32_MIN_NORMAL = 1.1754944e-38

BF16 = jnp.bfloat16
F32 = jnp.float32


def _dot(a, b):
    return jnp.dot(a, b, preferred_element_type=F32)


def _dot_nt(a, b):
    return lax.dot_general(a, b, (((1,), (1,)), ((), ())), preferred_element_type=F32)


def _dot_tn(a, b):
    return lax.dot_general(a, b, (((0,), (0,)), ((), ())), preferred_element_type=F32)


def _split(x):
    hi = x.astype(BF16)
    lo = (x - hi.astype(F32)).astype(BF16)
    return hi, lo


def _head_sums(x, ones2):
    outs = []
    for pi in range(x.shape[-1] // LANES):
        hi, lo = _split(x[:, pi * LANES:(pi + 1) * LANES])
        outs.append(_dot(jnp.concatenate([hi, lo], axis=1), ones2))
    return jnp.concatenate(outs, axis=1)


def _sigmoid(z):
    return 0.5 * jnp.tanh(0.5 * z) + 0.5


def _softplus(z):
    return jnp.maximum(z, 0.0) + jnp.log1p(jnp.exp(-jnp.abs(z)))


def _params(sem):
    return pltpu.CompilerParams(dimension_semantics=sem, vmem_limit_bytes=VMEM_LIMIT)


def _layer_slab(arr, layer, single_buffer=False):
    if arr.ndim == 2:
        return pl.BlockSpec(arr.shape, lambda *_: (0, 0))
    zeros = (0,) * (arr.ndim - 1)
    mode = {"pipeline_mode": pl.Buffered(1)} if single_buffer else {}
    return pl.BlockSpec((None,) + arr.shape[1:], lambda *_: (layer,) + zeros, **mode)


def _with_layer_rows(kernel_fn, layer, positions):
    def narrowed(*refs, **kwargs):
        refs = list(refs)
        for i in positions:
            refs[i] = refs[i].at[pl.ds(layer, 1)]
        return kernel_fn(*refs, **kwargs)
    return narrowed


def _mod_rows(mod, layer, batch_of):
    return pl.BlockSpec((None, 1, 1, mod.shape[-1]), lambda *g: (layer, batch_of(*g), 0, 0))


def _mod_kernel(c_ref, w_ref, b_ref, o_ref):
    c = c_ref[...]
    ca = c * jax.nn.sigmoid(c)
    w = w_ref[0]
    chi, clo = _split(ca)
    whi, wlo = _split(w)
    acc = _dot(chi, whi) + _dot(chi, wlo) + _dot(clo, whi)
    o_ref[0] = acc + b_ref[0]


def _modulation(c, w_mod, b_mod):
    depth, d, nd = w_mod.shape
    bsz = c.shape[0]
    nblk = nd // d
    return pl.pallas_call(
        _mod_kernel,
        grid=(depth, nblk),
        in_specs=[
            pl.BlockSpec((bsz, d), lambda l, j: (0, 0)),
            pl.BlockSpec((1, d, d), lambda l, j: (l, 0, j)),
            pl.BlockSpec((1, 1, d), lambda l, j: (l, 0, j)),
        ],
        out_specs=pl.BlockSpec((1, bsz, d), lambda l, j: (l, 0, j)),
        out_shape=jax.ShapeDtypeStruct((depth, bsz, nd), F32),
        compiler_params=_params(("arbitrary", "arbitrary")),
        name="adaln_mod",
    )(c, w_mod, b_mod.reshape(depth, 1, nd))


def _in_proj_kernel(x_ref, mod_ref, g_ref, w_ref, pa_ref, pb_ref, pc_ref, *, d, tile, row_block):
    na, nb = pa_ref.shape[-1], pb_ref.shape[-1]
    scale = 1.0 + mod_ref[0, :, d:2 * d]
    shift = mod_ref[0, :, 0:d]
    for r in range(tile // row_block):
        rows = slice(r * row_block, (r + 1) * row_block)
        x = x_ref[0, rows, :]
        y = x * lax.rsqrt(jnp.mean(x * x, axis=-1, keepdims=True) + EPS)
        h = (y * g_ref[...]) * scale + shift
        p = _dot(h.astype(BF16), w_ref[...])
        pa_ref[0, rows, :] = p[:, 0:na]
        pb_ref[0, rows, :] = p[:, na:na + nb]
        pc_ref[0, rows, :] = p[:, na + nb:]


def _in_proj(x, mod, g, w_in, widths, tile, layer):
    bsz, seq, d = x.shape
    na, nb, nc = widths
    return pl.pallas_call(
        functools.partial(_with_layer_rows(_in_proj_kernel, layer, (2,)), d=d, tile=tile,
                          row_block=min(tile, IN_PROJ_ROW_BLOCK)),
        grid=(bsz, seq // tile),
        in_specs=[
            pl.BlockSpec((1, tile, d), lambda b, t: (b, t, 0)),
            _mod_rows(mod, layer, lambda b, t: b),
            _layer_slab(g, layer),
            _layer_slab(w_in, layer, single_buffer=True),
        ],
        out_specs=[
            pl.BlockSpec((1, tile, na), lambda b, t: (b, t, 0)),
            pl.BlockSpec((1, tile, nb), lambda b, t: (b, t, 0)),
            pl.BlockSpec((1, tile, nc), lambda b, t: (b, t, 0)),
        ],
        out_shape=[
            jax.ShapeDtypeStruct((bsz, seq, na), F32),
            jax.ShapeDtypeStruct((bsz, seq, nb), F32),
            jax.ShapeDtypeStruct((bsz, seq, nc), F32),
        ],
        compiler_params=_params(("arbitrary", "arbitrary")),
        name="in_proj",
    )(x, mod, g, w_in)


def _sgu_kernel(p_ref, lng_ref, lnb_ref, w_ref, bias_ref, o_ref, *, d_a, n_chunks):
    z = jax.nn.gelu(p_ref[0])
    u = z[:, :d_a]
    v = z[:, d_a:]
    mu = jnp.mean(v, axis=-1, keepdims=True)
    var = jnp.mean(jnp.square(v - mu), axis=-1, keepdims=True)
    vn = ((v - mu) * lax.rsqrt(var + LN_EPS) * lng_ref[...] + lnb_ref[...]).astype(BF16)
    lane_lo = lax.broadcasted_iota(jnp.int32, (SGU_CHUNK, LANES), 1) < HEAD_DIM
    rows = []
    for c in range(n_chunks):
        cols = []
        for q in range(d_a // LANES):
            vq = vn[c * SGU_CHUNK:(c + 1) * SGU_CHUNK, q * LANES:(q + 1) * LANES]
            cols.append(_dot(w_ref[q], _stack_pair(vq, lane_lo)))
        rows.append(jnp.concatenate(cols, axis=1) + bias_ref[...])
    mixed = jnp.concatenate(rows, axis=0) if n_chunks > 1 else rows[0]
    o_ref[0] = (u * mixed).astype(o_ref.dtype)


def _sgu(p_a, ln_g, ln_b, w_masked, bias_cols, tile, layer):
    bsz, seq, two_da = p_a.shape
    d_a = two_da // 2
    return pl.pallas_call(
        functools.partial(_with_layer_rows(_sgu_kernel, layer, (1, 2)), d_a=d_a, n_chunks=tile // SGU_CHUNK),
        grid=(bsz, seq // tile),
        in_specs=[
            pl.BlockSpec((1, tile, two_da), lambda b, t: (b, t, 0)),
            _layer_slab(ln_g, layer), _layer_slab(ln_b, layer),
            _layer_slab(w_masked, layer), _layer_slab(bias_cols, layer),
        ],
        out_specs=pl.BlockSpec((1, tile, d_a), lambda b, t: (b, t, 0)),
        out_shape=jax.ShapeDtypeStruct((bsz, seq, d_a), BF16),
        compiler_params=_params(("arbitrary", "arbitrary")),
        name="sgu",
    )(p_a, ln_g, ln_b, w_masked, bias_cols)


def _rglru_kernel(p_ref, cw_ref, cb_ref, wbd_ref, bra_ref, bix_ref, lam_ref, o_ref,
                  xtail, hprev, *, d_b, tile):
    t = pl.program_id(1)

    @pl.when(t == 0)
    def _():
        xtail[...] = jnp.zeros_like(xtail)
        hprev[...] = jnp.zeros_like(hprev)

    n_grp = tile // SUBLANES
    sub = lax.broadcasted_iota(jnp.int32, (n_grp, SUBLANES, d_b), 1)

    x_new = p_ref[0, :, 0:d_b]
    yg = p_ref[0, :, d_b:2 * d_b]
    x_ext = jnp.concatenate([xtail[...], x_new], axis=0).reshape(n_grp + 1, SUBLANES, d_b)
    xtail[...] = x_new[tile - SUBLANES:tile, :]
    xr3 = cb_ref[...] + cw_ref[CONV_B - 1:CONV_B, :] * x_ext[1:]
    for k in range(1, CONV_B):
        rot = pltpu.roll(x_ext, k, 1)
        xr3 = xr3 + cw_ref[CONV_B - 1 - k:CONV_B - k, :] * jnp.where(sub >= k, rot[1:], rot[:-1])
    xr = xr3.reshape(tile, d_b)

    ri = _dot(xr.astype(BF16), wbd_ref[...])
    r = _sigmoid(ri[:, 0:d_b] + bra_ref[...])
    i = _sigmoid(ri[:, d_b:2 * d_b] + bix_ref[...])
    log_a = (-LRU_C * r) * _softplus(-lam_ref[...])
    a = jnp.exp(log_a)
    one_minus_a2 = -jnp.tanh(log_a) * (a * a + 1.0)
    bterm = (one_minus_a2 * lax.rsqrt(jnp.maximum(one_minus_a2, F32_MIN_NORMAL))) * (i * xr)

    a3 = a.reshape(n_grp, SUBLANES, d_b)
    b3 = bterm.reshape(n_grp, SUBLANES, d_b)
    shift = 1
    while shift < SUBLANES:
        keep = sub >= shift
        a_s = jnp.where(keep, pltpu.roll(a3, shift, 1), 1.0)
        b_s = jnp.where(keep, pltpu.roll(b3, shift, 1), 0.0)
        b3 = a3 * b_s + b3
        a3 = a3 * a_s
        shift *= 2
    h = hprev[...]
    groups = []
    for g in range(n_grp):
        hg = b3[g] + a3[g] * h
        groups.append(hg)
        h = hg[SUBLANES - 1:SUBLANES, :]
    hprev[...] = h
    o_ref[0] = (jax.nn.gelu(yg) * jnp.concatenate(groups, axis=0)).astype(o_ref.dtype)


def _rglru(p_b, conv_w, conv_b, w_bd, b_ra, b_ix, lam, tile, layer):
    bsz, seq, two_db = p_b.shape
    d_b = two_db // 2
    return pl.pallas_call(
        functools.partial(_with_layer_rows(_rglru_kernel, layer, (2, 4, 5, 6)), d_b=d_b, tile=tile),
        grid=(bsz, seq // tile),
        in_specs=[pl.BlockSpec((1, tile, two_db), lambda b, t: (b, t, 0))]
        + [_layer_slab(a, layer) for a in (conv_w, conv_b, w_bd, b_ra, b_ix, lam)],
        out_specs=pl.BlockSpec((1, tile, d_b), lambda b, t: (b, t, 0)),
        out_shape=jax.ShapeDtypeStruct((bsz, seq, d_b), BF16),
        scratch_shapes=[
            pltpu.VMEM((SUBLANES, d_b), F32),
            pltpu.VMEM((1, d_b), F32),
        ],
        compiler_params=_params(("arbitrary", "arbitrary")),
        name="rglru",
    )(p_b, conv_w, conv_b, w_bd, b_ra, b_ix, lam)


def _stack_pair(x, lane_lo):
    return jnp.concatenate([jnp.where(lane_lo, x, 0.0), jnp.where(lane_lo, 0.0, x)], axis=0)


def _unstack_pair(x):
    return x[0:RWKV_CHUNK] + x[RWKV_CHUNK:2 * RWKV_CHUNK]


RWKV_STAGED = ("kt", "kh", "bh", "khl", "bhl", "v", "rt", "gl", "gate", "bonus")
RWKV_STAGED_BF16 = 6
RWKV_TRACE_ORDER = "cpccpccpccp"
RWKV_CARRY_DTYPE = BF16


def _rwkv_prep(p_ref, mu_ref, w0_ref, a0_ref, kk_ref, ka_ref, rk_ref, wl_ref, g2_ref, ones_ref, xbuf, dst,
               *, d_c, tile):
    n_chunks = tile // RWKV_CHUNK
    xbuf[SUBLANES:SUBLANES + tile, :] = p_ref[0]
    p = p_ref[0]
    prev = xbuf[pl.ds(SUBLANES - 1, tile), :]
    xbuf[0:SUBLANES, :] = xbuf[tile:tile + SUBLANES, :]
    ps = p + (prev - p) * mu_ref[...]

    r = ps[:, 0:d_c]
    k = ps[:, d_c:2 * d_c]
    v = ps[:, 2 * d_c:3 * d_c]
    xwa = ps[:, 3 * d_c:3 * d_c + LORA_W + LORA_A]
    xg = ps[:, 3 * d_c + LORA_W + LORA_A:]

    lane_l = lax.broadcasted_iota(jnp.int32, xwa.shape, 1)
    lin = jnp.where(lane_l < LORA_W, jnp.tanh(xwa), xwa).astype(BF16)
    wa_lora = _dot(lin, wl_ref[...])
    w = -_softplus(-(w0_ref[...] + wa_lora[:, 0:d_c])) - 0.5
    lw = -jnp.exp(w)
    a = jax.nn.sigmoid(a0_ref[...] + wa_lora[:, d_c:2 * d_c])
    dst["gate"][...] = _dot(jax.nn.sigmoid(xg).astype(BF16), g2_ref[...])

    yield
    ones2 = ones_ref[...]
    kk = k * kk_ref[...]
    kappa = kk * lax.rsqrt(jnp.maximum(_head_sums(kk * kk, ones2), 1e-24))
    kmod = k * (1.0 + (a - 1.0) * ka_ref[...])
    bvec = kappa * a
    yield
    dst["bonus"][...] = _head_sums(r * kmod * rk_ref[...], ones2) * v

    ri = lax.broadcasted_iota(jnp.int32, (LANES, 2 * LANES), 0)
    ci = lax.broadcasted_iota(jnp.int32, (LANES, 2 * LANES), 1) & (LANES - 1)
    shift = RWKV_CHUNK.bit_length() - 1
    tril2 = jnp.where((jnp.right_shift(ri, shift) == jnp.right_shift(ci, shift)) & (ci <= ri),
                      1.0, 0.0).astype(BF16)
    lw_hi, lw_lo = _split(lw)
    c_parts = []
    for m in range(tile // LANES):
        rws = slice(m * LANES, (m + 1) * LANES)
        c_parts.append(_dot(tril2, jnp.concatenate([lw_hi[rws], lw_lo[rws]], axis=0)))
    c_incl = jnp.concatenate(c_parts, axis=0) if len(c_parts) > 1 else c_parts[0]
    c_tot = jnp.concatenate(
        [jnp.broadcast_to(c_incl[(c + 1) * RWKV_CHUNK - 1:(c + 1) * RWKV_CHUNK, :], (RWKV_CHUNK, d_c))
         for c in range(n_chunks)], axis=0)

    yield
    e_neg = jnp.exp(-c_incl)
    e_end = jnp.exp(c_tot - c_incl)
    dst["kt"][...] = (kappa * jnp.exp(c_incl - lw)).astype(BF16)
    dst["rt"][...] = r * jnp.exp(c_incl)
    dst["kh"][...] = (kmod * e_neg).astype(BF16)
    dst["bh"][...] = (bvec * e_neg).astype(BF16)
    dst["khl"][...] = (kmod * e_end).astype(BF16)
    dst["bhl"][...] = (bvec * e_end).astype(BF16)
    dst["v"][...] = v.astype(BF16)
    dst["gl"][...] = jnp.exp(c_tot)


def _rwkv_chains(src, rbar_ref, mmat_ref, obar_ref, gbar_ref, gate_ref, bonus_ref, *, d_c, tile):
    n_chunks = tile // RWKV_CHUNK
    n_pairs = d_c // LANES
    gate_ref[0] = src["gate"][...].astype(gate_ref.dtype)
    bonus_ref[0] = src["bonus"][...].astype(bonus_ref.dtype)

    ii = lax.broadcasted_iota(jnp.int32, (LANES, LANES), 0)
    jj = lax.broadcasted_iota(jnp.int32, (LANES, LANES), 1)
    strict = ii > jj
    incl = ii >= jj
    eye = ii == jj
    lane_lo = lax.broadcasted_iota(jnp.int32, (RWKV_CHUNK, LANES), 1) < HEAD_DIM

    chains = [(c, pi) for c in range(n_chunks) for pi in range(n_pairs)]

    def rows_of(ch):
        return slice(ch[0] * RWKV_CHUNK, (ch[0] + 1) * RWKV_CHUNK)

    def lanes_of(ch):
        return slice(ch[1] * LANES, (ch[1] + 1) * LANES)

    def stacked(name, ch):
        return _stack_pair(src[name][rows_of(ch), lanes_of(ch)], lane_lo)

    rts = [stacked("rt", ch) for ch in chains]
    kts_b = [stacked("kt", ch) for ch in chains]
    vbd = [stacked("v", ch) for ch in chains]

    sc = [_dot_nt(jnp.concatenate([kts_b[n], rts[n].astype(BF16)], axis=0),
                  jnp.concatenate([stacked("kh", ch), stacked("bh", ch)], axis=0))
          for n, ch in enumerate(chains)]
    a_k = [jnp.where(strict, s[0:LANES, 0:LANES], 0.0).astype(BF16) for s in sc]
    n_b = [jnp.where(strict, s[0:LANES, LANES:2 * LANES], 0.0) for s in sc]
    p_m = [jnp.where(incl, s[LANES:2 * LANES, 0:LANES], 0.0).astype(BF16) for s in sc]
    q_m = [jnp.where(incl, s[LANES:2 * LANES, LANES:2 * LANES], 0.0).astype(BF16) for s in sc]
    yield

    x_m = [jnp.where(eye, 1.0, 0.0) - n for n in n_b]
    pw = [_dot(n.astype(BF16), n.astype(BF16)) for n in n_b]
    yield
    for _ in range(4):
        y = [_dot(jnp.concatenate([x.astype(BF16), p.astype(BF16)], axis=0), p.astype(BF16))
             for x, p in zip(x_m, pw)]
        x_m = [x + yy[0:LANES] for x, yy in zip(x_m, y)]
        pw = [yy[LANES:2 * LANES] for yy in y]
        yield
    t_m = [(x + _dot(x.astype(BF16), p.astype(BF16))).astype(BF16) for x, p in zip(x_m, pw)]
    yield

    akv = [_dot(a, vv) for a, vv in zip(a_k, vbd)]
    yield
    ty_b = [_dot(tm, jnp.concatenate([kb, a.astype(BF16)], axis=1)).astype(BF16)
            for tm, kb, a in zip(t_m, kts_b, akv)]
    yield
    rhs = [jnp.concatenate([jnp.concatenate([jnp.zeros_like(vv), vv], axis=1), tyb], axis=0)
           for vv, tyb in zip(vbd, ty_b)]
    tok = [_dot(jnp.concatenate([p, -q], axis=1), rr) for p, q, rr in zip(p_m, q_m, rhs)]
    chn = [_dot_tn(jnp.concatenate([stacked("khl", ch), -stacked("bhl", ch)], axis=0), rr)
           for ch, rr in zip(chains, rhs)]
    yield

    for n, ch in enumerate(chains):
        rows, lanes = rows_of(ch), lanes_of(ch)
        rbar = stacked("rt", ch) + tok[n][:, 0:LANES]
        gl_row = src["gl"][ch[0] * RWKV_CHUNK:ch[0] * RWKV_CHUNK + 1, lanes]
        mmat = jnp.where(eye, gl_row, 0.0) + chn[n][:, 0:LANES]
        rbar_ref[0, rows, lanes] = _unstack_pair(rbar).astype(rbar_ref.dtype)
        mmat_ref[0, rows, lanes] = _unstack_pair(mmat).astype(mmat_ref.dtype)
        obar_ref[0, rows, lanes] = _unstack_pair(tok[n][:, LANES:2 * LANES]).astype(obar_ref.dtype)
        gbar_ref[0, rows, lanes] = _unstack_pair(chn[n][:, LANES:2 * LANES]).astype(gbar_ref.dtype)


def _rwkv_chunk_kernel(p_ref, mu_ref, w0_ref, a0_ref, kk_ref, ka_ref, rk_ref, wl_ref, g2_ref, ones_ref,
                       rbar_ref, mmat_ref, obar_ref, gbar_ref, gate_ref, bonus_ref,
                       xbuf, *staged, d_c, tile, tiles_per_seq):
    s = pl.program_id(0)
    n = len(RWKV_STAGED)
    set_a = dict(zip(RWKV_STAGED, staged[:n]))
    set_b = dict(zip(RWKV_STAGED, staged[n:]))

    @pl.when(s == 0)
    def _():
        for ref in staged[n:]:
            ref[...] = jnp.zeros_like(ref)

    @pl.when(lax.rem(s, tiles_per_seq) == 0)
    def _():
        xbuf[0:SUBLANES, :] = jnp.zeros((SUBLANES, xbuf.shape[-1]), F32)

    def step(dst, src):
        prep = _rwkv_prep(p_ref, mu_ref, w0_ref, a0_ref, kk_ref, ka_ref, rk_ref, wl_ref, g2_ref, ones_ref, xbuf,
                          dst, d_c=d_c, tile=tile)
        chains = _rwkv_chains(src, rbar_ref, mmat_ref, obar_ref, gbar_ref, gate_ref, bonus_ref,
                              d_c=d_c, tile=tile)
        for who in RWKV_TRACE_ORDER:
            next(prep if who == "p" else chains, None)
        for _ in chains:
            pass
        for _ in prep:
            pass

    parity = lax.rem(s, 2)

    @pl.when(parity == 0)
    def _():
        step(set_a, set_b)

    @pl.when(parity == 1)
    def _():
        step(set_b, set_a)


def _rwkv_chunk(p_c, mu, w0, a0, k_k, k_a, r_k, w_lora, g2, ones2, tile, layer):
    bsz, seq, pc = p_c.shape
    d_c = w0.shape[-1]
    n_t = seq // tile
    n_tiles = bsz * n_t

    def staged_tile(s):
        q = jnp.minimum(s, n_tiles - 1)
        return (q // n_t, q % n_t, 0)

    def finished_tile(s):
        q = jnp.maximum(s - 1, 0)
        return (q // n_t, q % n_t, 0)

    tok = pl.BlockSpec((1, tile, d_c), finished_tile)
    shp = lambda dt: jax.ShapeDtypeStruct((bsz, seq, d_c), dt)
    one_set = ([pltpu.VMEM((tile, d_c), BF16)] * RWKV_STAGED_BF16
               + [pltpu.VMEM((tile, d_c), F32)] * (len(RWKV_STAGED) - RWKV_STAGED_BF16))
    return pl.pallas_call(
        functools.partial(_with_layer_rows(_rwkv_chunk_kernel, layer, (1, 2, 3, 4, 5, 6)), d_c=d_c, tile=tile,
                          tiles_per_seq=n_t),
        grid=(n_tiles + 1,),
        in_specs=[pl.BlockSpec((1, tile, pc), staged_tile)]
        + [_layer_slab(a, layer) for a in (mu, w0, a0, k_k, k_a, r_k, w_lora, g2)]
        + [pl.BlockSpec(ones2.shape, lambda s: (0, 0))],
        out_specs=[tok, tok, tok, tok, tok, tok],
        out_shape=[shp(RWKV_CARRY_DTYPE)] * 6,
        scratch_shapes=[pltpu.VMEM((tile + SUBLANES, pc), F32)] + one_set + one_set,
        compiler_params=_params(("arbitrary",)),
        name="rwkv_chunk",
    )(p_c, mu, w0, a0, k_k, k_a, r_k, w_lora, g2, ones2)


def _rwkv_scan_kernel(rbar_ref, mmat_ref, obar_ref, gbar_ref, gate_ref, bonus_ref, lnw_ref, lnb_ref, ones_ref,
                      y_ref, h_s, o_s, *, d_c, tile):
    t = pl.program_id(0)
    bsz = obar_ref.shape[0]
    n_chunks = tile // RWKV_CHUNK
    n_pairs = d_c // LANES

    @pl.when(t == 0)
    def _():
        h_s[...] = jnp.zeros_like(h_s)

    lane_lo = lax.broadcasted_iota(jnp.int32, (RWKV_CHUNK, LANES), 1) < HEAD_DIM
    chains = [(b, pi) for b in range(bsz) for pi in range(n_pairs)]

    ones2 = ones_ref[...]
    inv = 1.0 / HEAD_DIM

    def finish(c):
        rows = slice(c * RWKV_CHUNK, (c + 1) * RWKV_CHUNK)
        o = o_s[:, rows, :].reshape(bsz * RWKV_CHUNK, d_c)
        mean = _head_sums(o, ones2) * inv
        dlt = o - mean
        var = _head_sums(dlt * dlt, ones2) * inv
        gn = (dlt * lax.rsqrt(var + GN_EPS) * lnw_ref[...] + lnb_ref[...]).reshape(bsz, RWKV_CHUNK, d_c)
        y_ref[:, rows, :] = ((gn + bonus_ref[:, rows, :]) * gate_ref[:, rows, :]).astype(y_ref.dtype)

    for c in range(n_chunks):
        rows = slice(c * RWKV_CHUNK, (c + 1) * RWKV_CHUNK)
        outs = []
        for n, (b, pi) in enumerate(chains):
            lanes = slice(pi * LANES, (pi + 1) * LANES)
            hbd = _stack_pair(h_s[n], lane_lo).astype(BF16)
            lhs = jnp.concatenate([rbar_ref[b, rows, lanes], mmat_ref[b, rows, lanes]], axis=0)
            outs.append(_dot(lhs, hbd))
        if c > 0:
            finish(c - 1)
        for n, (b, pi) in enumerate(chains):
            lanes = slice(pi * LANES, (pi + 1) * LANES)
            o_s[b, rows, lanes] = outs[n][0:RWKV_CHUNK] + obar_ref[b, rows, lanes]
            h_s[n] = outs[n][RWKV_CHUNK:2 * RWKV_CHUNK] + gbar_ref[b, rows, lanes]
    finish(n_chunks - 1)


def _rwkv_scan(rbar, mmat, obar, gbar, gate, bonus, ln_w, ln_b, ones2, tile, layer):
    bsz, seq, d_c = obar.shape
    tok = pl.BlockSpec((bsz, tile, d_c), lambda t: (0, t, 0))
    return pl.pallas_call(
        functools.partial(_with_layer_rows(_rwkv_scan_kernel, layer, (6, 7)), d_c=d_c, tile=tile),
        grid=(seq // tile,),
        in_specs=[tok, tok, tok, tok, tok, tok, _layer_slab(ln_w, layer), _layer_slab(ln_b, layer),
                  pl.BlockSpec(ones2.shape, lambda t: (0, 0))],
        out_specs=tok,
        out_shape=jax.ShapeDtypeStruct((bsz, seq, d_c), BF16),
        scratch_shapes=[
            pltpu.VMEM((bsz * (d_c // LANES), RWKV_CHUNK, LANES), F32),
            pltpu.VMEM((bsz, tile, d_c), F32),
        ],
        compiler_params=_params(("arbitrary",)),
        name="rwkv_scan",
    )(rbar, mmat, obar, gbar, gate, bonus, ln_w, ln_b, ones2)


def _mixer_out(x_ref, ya_ref, yb_ref, yc_ref, mod_ref, wo_ref, g_ref, dst, *, d):
    y = _dot(jnp.concatenate([ya_ref[0], yb_ref[0], yc_ref[0]], axis=1), wo_ref[...])
    yield
    x1 = x_ref[0] + mod_ref[0, :, 2 * d:3 * d] * y
    dst["x1"][...] = x1
    n = x1 * lax.rsqrt(jnp.mean(x1 * x1, axis=-1, keepdims=True) + EPS)
    h2 = (n * g_ref[...]) * (1.0 + mod_ref[0, :, 4 * d:5 * d]) + mod_ref[0, :, 3 * d:4 * d]
    dst["h2"][...] = h2.astype(BF16)


def _conv_glu_ffn(src, mod_ref, wup_ref, cw_ref, cb_ref, wd_ref, gf_ref, o_ref, gbuf, gtail,
                  *, d, tile, d_ff, splits, final_norm):
    h2 = src["h2"][...]
    mids = []
    for lo, hi in splits:
        gbuf[0:SUBLANES, lo:hi] = gtail[:, lo:hi]
        gbuf[SUBLANES:SUBLANES + tile, lo:hi] = _dot(h2, wup_ref[:, lo:hi])
        yield
        val = _dot(h2, wup_ref[:, d_ff + lo:d_ff + hi])
        gc = cb_ref[:, lo:hi] + cw_ref[0:1, lo:hi] * gbuf[pl.ds(SUBLANES - (CONV_FF - 1), tile), lo:hi]
        for q in range(1, CONV_FF):
            gc = gc + cw_ref[q:q + 1, lo:hi] * gbuf[pl.ds(SUBLANES - (CONV_FF - 1) + q, tile), lo:hi]
        gtail[:, lo:hi] = gbuf[tile:tile + SUBLANES, lo:hi]
        mids.append((gc * jax.nn.sigmoid(gc) * val).astype(BF16))
        yield
    ffn = _dot(mids[0], wd_ref[splits[0][0]:splits[0][1], :])
    for mid, (lo, hi) in zip(mids[1:], splits[1:]):
        ffn = ffn + _dot(mid, wd_ref[lo:hi, :])
    x2 = src["x1"][...] + mod_ref[0, :, 5 * d:6 * d] * ffn
    if final_norm:
        x2 = x2 * lax.rsqrt(jnp.mean(x2 * x2, axis=-1, keepdims=True) + EPS) * gf_ref[...]
    o_ref[0] = x2


def _out_ffn_kernel(x_ref, ya_ref, yb_ref, yc_ref, mod_in_ref, mod_out_ref, wo_ref, g_ref,
                    wup_ref, cw_ref, cb_ref, wd_ref, gf_ref, o_ref,
                    gbuf, gtail, x1_a, h2_a, x1_b, h2_b, *, d, tile, d_ff, splits, tiles_per_seq, final_norm):
    s = pl.program_id(0)
    set_a = {"x1": x1_a, "h2": h2_a}
    set_b = {"x1": x1_b, "h2": h2_b}

    @pl.when(s == 0)
    def _():
        x1_b[...] = jnp.zeros_like(x1_b)
        h2_b[...] = jnp.zeros_like(h2_b)

    @pl.when(lax.rem(jnp.maximum(s - 1, 0), tiles_per_seq) == 0)
    def _():
        gtail[...] = jnp.zeros_like(gtail)

    def step(dst, src):
        head = _mixer_out(x_ref, ya_ref, yb_ref, yc_ref, mod_in_ref, wo_ref, g_ref, dst, d=d)
        ffn = _conv_glu_ffn(src, mod_out_ref, wup_ref, cw_ref, cb_ref, wd_ref, gf_ref, o_ref, gbuf, gtail,
                            d=d, tile=tile, d_ff=d_ff, splits=splits, final_norm=final_norm)
        for who in OUT_FFN_TRACE_ORDER:
            next(head if who == "h" else ffn, None)
        for _ in ffn:
            pass
        for _ in head:
            pass

    parity = lax.rem(s, 2)

    @pl.when(parity == 0)
    def _():
        step(set_a, set_b)

    @pl.when(parity == 1)
    def _():
        step(set_b, set_a)


def _column_splits(width):
    half = (width // (2 * MXU_TILE)) * MXU_TILE or width // 2
    return ((0, width - half), (width - half, width))


def _out_ffn(x, y_a, y_b, y_c, mod, w_out, g_ffn, w_up, conv_w, conv_b, w_down, g_final, tile, final_norm,
             layer):
    bsz, seq, d = x.shape
    d_ff = w_down.shape[1]
    n_t = seq // tile
    n_tiles = bsz * n_t
    resident = lambda a: _layer_slab(a, layer, single_buffer=True)

    def staged(s):
        q = jnp.minimum(s, n_tiles - 1)
        return q // n_t, q % n_t

    def finished(s):
        q = jnp.maximum(s - 1, 0)
        return q // n_t, q % n_t

    tok_in = lambda w: pl.BlockSpec((1, tile, w), lambda s: (*staged(s), 0))
    return pl.pallas_call(
        functools.partial(_with_layer_rows(_out_ffn_kernel, layer, (7, 10)), d=d, tile=tile, d_ff=d_ff,
                          splits=_column_splits(d_ff),
                          tiles_per_seq=n_t, final_norm=final_norm),
        grid=(n_tiles + 1,),
        in_specs=[
            tok_in(d), tok_in(y_a.shape[-1]), tok_in(y_b.shape[-1]), tok_in(y_c.shape[-1]),
            _mod_rows(mod, layer, lambda s: staged(s)[0]), _mod_rows(mod, layer, lambda s: finished(s)[0]),
            resident(w_out),
            _layer_slab(g_ffn, layer),
            resident(w_up),
            _layer_slab(conv_w, layer),
            _layer_slab(conv_b, layer),
            resident(w_down),
            pl.BlockSpec((1, d), lambda s: (0, 0)),
        ],
        out_specs=pl.BlockSpec((1, tile, d), lambda s: (*finished(s), 0)),
        out_shape=jax.ShapeDtypeStruct((bsz, seq, d), F32),
        scratch_shapes=[
            pltpu.VMEM((tile + SUBLANES, d_ff), F32),
            pltpu.VMEM((SUBLANES, d_ff), F32),
            pltpu.VMEM((tile, d), F32), pltpu.VMEM((tile, d), BF16),
            pltpu.VMEM((tile, d), F32), pltpu.VMEM((tile, d), BF16),
        ],
        compiler_params=_params(("arbitrary",)),
        name="out_ffn",
    )(x, y_a, y_b, y_c, mod, mod, w_out, g_ffn, w_up, conv_w, conv_b, w_down, g_final)


def _block_diag(w):
    h, n, _ = w.shape
    eye = jnp.eye(h, dtype=w.dtype)
    return (eye[:, None, :, None] * w[:, :, None, :]).reshape(h * n, h * n)


def _pick_tile(seq, want):
    tile = min(seq, want)
    while seq % tile:
        tile //= 2
    return tile


def kernel(x, c, w_mod, b_mod, norm_mix, w_in, w_out, sgu_ln_g, sgu_ln_b, sgu_w, sgu_b, lru_conv_w, lru_conv_b, lru_w_a, lru_b_a, lru_w_x, lru_b_x, lru_lambda, rwkv_mu, rwkv_w0, rwkv_w2, rwkv_a0, rwkv_a2, rwkv_g2, rwkv_k_k, rwkv_k_a, rwkv_r_k, rwkv_ln_w, rwkv_ln_b, norm_ffn, ffn_w_up, ffn_conv_w, ffn_conv_b, ffn_w_down, norm_final):
    bsz, seq, d = x.shape
    depth = w_in.shape[0]
    d_a = sgu_ln_g.shape[-1]
    d_b = lru_conv_b.shape[-1]
    d_c = rwkv_w0.shape[-1]
    p_a, p_b = 2 * d_a, 2 * d_b

    t_proj = _pick_tile(seq, 1024)
    t_sgu = _pick_tile(seq, 2048)
    t_lru = _pick_tile(seq, 512)
    t_chunk = _pick_tile(seq, 256)
    t_scan = _pick_tile(seq, 256)
    t_ffn = _pick_tile(seq, 512)

    mod = _modulation(c, w_mod, b_mod).reshape(depth, bsz, 1, N_MOD * d)
    ones_pair = _block_diag(jnp.ones((LANES // HEAD_DIM, HEAD_DIM, HEAD_DIM), BF16))
    ones2 = jnp.concatenate([ones_pair, ones_pair], axis=0)
    tril = jnp.tril(jnp.ones((SGU_CHUNK, SGU_CHUNK), dtype=bool))
    w_in_b = w_in.astype(BF16)
    w_out_b = w_out.astype(BF16)
    w_up_b = ffn_w_up.astype(BF16)
    w_down_b = ffn_w_down.astype(BF16)
    heads_per_group = LANES // HEAD_DIM
    sgu_wm = jnp.where(tril, sgu_w, 0.0).astype(BF16)
    sgu_wm = sgu_wm.reshape(depth, -1, heads_per_group, SGU_CHUNK, SGU_CHUNK)
    sgu_wm = jnp.swapaxes(sgu_wm, 2, 3).reshape(depth, -1, SGU_CHUNK, heads_per_group * SGU_CHUNK)
    sgu_bias = jnp.repeat(jnp.swapaxes(sgu_b, 1, 2), HEAD_DIM, axis=2)
    block_diag_l = jax.vmap(_block_diag)
    lru_w_bd = jnp.concatenate([block_diag_l(lru_w_a), block_diag_l(lru_w_x)], axis=2).astype(BF16)
    zeros_l = jnp.zeros((depth, LORA_W, d_c), F32)
    w_lora = jnp.concatenate([
        jnp.concatenate([rwkv_w2, zeros_l], axis=2),
        jnp.concatenate([zeros_l, rwkv_a2], axis=2)], axis=1).astype(BF16)
    g2_b = rwkv_g2.astype(BF16)

    r_k = rwkv_r_k.reshape(depth, d_c)

    for l in range(depth):
        pa, pb, pc = _in_proj(x, mod, norm_mix, w_in_b, (p_a, p_b, w_in.shape[-1] - p_a - p_b), t_proj, l)
        y_a = _sgu(pa, sgu_ln_g, sgu_ln_b, sgu_wm, sgu_bias, t_sgu, l)
        y_b = _rglru(pb, lru_conv_w, lru_conv_b, lru_w_bd, lru_b_a, lru_b_x, lru_lambda, t_lru, l)
        rbar, mmat, obar, gbar, gate, bonus = _rwkv_chunk(
            pc, rwkv_mu, rwkv_w0, rwkv_a0, rwkv_k_k, rwkv_k_a, r_k, w_lora, g2_b, ones2, t_chunk, l)
        y_c = _rwkv_scan(rbar, mmat, obar, gbar, gate, bonus, rwkv_ln_w, rwkv_ln_b, ones2, t_scan, l)
        x = _out_ffn(x, y_a, y_b, y_c, mod, w_out_b, norm_ffn, w_up_b, ffn_conv_w, ffn_conv_b,
                     w_down_b, norm_final.reshape(1, -1), t_ffn, l == depth - 1, l)
    return x
```

```python
import functools

import jax
import jax.numpy as jnp
from jax import lax
from jax.experimental import pallas as pl
from jax.experimental.pallas import tpu as pltpu

HEAD_DIM = 64
SGU_CHUNK = 128
CONV_B = 4
LRU_C = 8.0
LORA_W = 64
LORA_A = 64
LORA_G = 128
CONV_FF = 3
N_MOD = 6
EPS = 1e-6
LN_EPS = 1e-5
GN_EPS = 64e-5

RWKV_CHUNK = 64
LANES = 128
MXU_TILE = 256
IN_PROJ_ROW_BLOCK = 128
OUT_FFN_TRACE_ORDER = "hfhfff"
SUBLANES = 8
VMEM_LIMIT = 56 * 1024 * 1024
F32_MIN_NORMAL = 1.1754944e-38

BF16 = jnp.bfloat16
F32 = jnp.float32


def _dot(a, b):
    return jnp.dot(a, b, preferred_element_type=F32)


def _dot_nt(a, b):
    return lax.dot_general(a, b, (((1,), (1,)), ((), ())), preferred_element_type=F32)


def _dot_tn(a, b):
    return lax.dot_general(a, b, (((0,), (0,)), ((), ())), preferred_element_type=F32)


def _split(x):
    hi = x.astype(BF16)
    lo = (x - hi.astype(F32)).astype(BF16)
    return hi, lo


def _head_sums(x, ones2, split=True):
    outs = []
    for pi in range(x.shape[-1] // LANES):
        xp = x[:, pi * LANES:(pi + 1) * LANES]
        if split:
            hi, lo = _split(xp)
            outs.append(_dot(jnp.concatenate([hi, lo], axis=1), ones2))
        else:
            outs.append(_dot(xp.astype(BF16), ones2[0:LANES]))
    return jnp.concatenate(outs, axis=1)


def _sigmoid(z):
    return 0.5 * jnp.tanh(0.5 * z) + 0.5


def _softplus(z, small_values_matter=True):
    t = jnp.exp(-jnp.abs(z))
    return jnp.maximum(z, 0.0) + (jnp.log1p(t) if small_values_matter else jnp.log(1.0 + t))


def _params(sem):
    return pltpu.CompilerParams(dimension_semantics=sem, vmem_limit_bytes=VMEM_LIMIT)


def _layer_slab(arr, layer, single_buffer=False):
    if arr.ndim == 2:
        return pl.BlockSpec(arr.shape, lambda *_: (0, 0))
    zeros = (0,) * (arr.ndim - 1)
    mode = {"pipeline_mode": pl.Buffered(1)} if single_buffer else {}
    return pl.BlockSpec((None,) + arr.shape[1:], lambda *_: (layer,) + zeros, **mode)


def _with_layer_rows(kernel_fn, layer, positions):
    def narrowed(*refs, **kwargs):
        refs = list(refs)
        for i in positions:
            refs[i] = refs[i].at[pl.ds(layer, 1)]
        return kernel_fn(*refs, **kwargs)
    return narrowed


def _mod_rows(mod, layer, batch_of):
    return pl.BlockSpec((None, 1, 1, mod.shape[-1]), lambda *g: (layer, batch_of(*g), 0, 0))


def _mod_kernel(c_ref, w_ref, b_ref, o_ref):
    c = c_ref[...]
    ca = c * jax.nn.sigmoid(c)
    w = w_ref[0]
    chi, clo = _split(ca)
    whi, wlo = _split(w)
    acc = _dot(chi, whi) + _dot(chi, wlo) + _dot(clo, whi)
    o_ref[0] = acc + b_ref[0]


def _modulation(c, w_mod, b_mod):
    depth, d, nd = w_mod.shape
    bsz = c.shape[0]
    nblk = nd // d
    return pl.pallas_call(
        _mod_kernel,
        grid=(depth, nblk),
        in_specs=[
            pl.BlockSpec((bsz, d), lambda l, j: (0, 0)),
            pl.BlockSpec((1, d, d), lambda l, j: (l, 0, j)),
            pl.BlockSpec((1, 1, d), lambda l, j: (l, 0, j)),
        ],
        out_specs=pl.BlockSpec((1, bsz, d), lambda l, j: (l, 0, j)),
        out_shape=jax.ShapeDtypeStruct((depth, bsz, nd), F32),
        compiler_params=_params(("arbitrary", "arbitrary")),
        name="adaln_mod",
    )(c, w_mod, b_mod.reshape(depth, 1, nd))


def _in_proj_kernel(x_ref, mod_ref, g_ref, w_ref, pa_ref, pb_ref, pc_ref, *, d, tile, row_block):
    na, nb = pa_ref.shape[-1], pb_ref.shape[-1]
    scale = 1.0 + mod_ref[0, :, d:2 * d]
    shift = mod_ref[0, :, 0:d]
    for r in range(tile // row_block):
        rows = slice(r * row_block, (r + 1) * row_block)
        x = x_ref[0, rows, :]
        y = x * lax.rsqrt(jnp.mean(x * x, axis=-1, keepdims=True) + EPS)
        h = (y * g_ref[...]) * scale + shift
        p = _dot(h.astype(BF16), w_ref[...])
        pa_ref[0, rows, :] = p[:, 0:na]
        pb_ref[0, rows, :] = p[:, na:na + nb]
        pc_ref[0, rows, :] = p[:, na + nb:]


def _in_proj(x, mod, g, w_in, widths, tile, layer):
    bsz, seq, d = x.shape
    na, nb, nc = widths
    return pl.pallas_call(
        functools.partial(_with_layer_rows(_in_proj_kernel, layer, (2,)), d=d, tile=tile,
                          row_block=min(tile, IN_PROJ_ROW_BLOCK)),
        grid=(bsz, seq // tile),
        in_specs=[
            pl.BlockSpec((1, tile, d), lambda b, t: (b, t, 0)),
            _mod_rows(mod, layer, lambda b, t: b),
            _layer_slab(g, layer),
            _layer_slab(w_in, layer, single_buffer=True),
        ],
        out_specs=[
            pl.BlockSpec((1, tile, na), lambda b, t: (b, t, 0)),
            pl.BlockSpec((1, tile, nb), lambda b, t: (b, t, 0)),
            pl.BlockSpec((1, tile, nc), lambda b, t: (b, t, 0)),
        ],
        out_shape=[
            jax.ShapeDtypeStruct((bsz, seq, na), F32),
            jax.ShapeDtypeStruct((bsz, seq, nb), F32),
            jax.ShapeDtypeStruct((bsz, seq, nc), F32),
        ],
        compiler_params=_params(("arbitrary", "arbitrary")),
        name="in_proj",
    )(x, mod, g, w_in)


def _sgu_kernel(p_ref, lng_ref, lnb_ref, w_ref, bias_ref, o_ref, *, d_a, n_chunks):
    z = jax.nn.gelu(p_ref[0])
    u = z[:, :d_a]
    v = z[:, d_a:]
    mu = jnp.mean(v, axis=-1, keepdims=True)
    var = jnp.mean(jnp.square(v - mu), axis=-1, keepdims=True)
    vn = ((v - mu) * lax.rsqrt(var + LN_EPS) * lng_ref[...] + lnb_ref[...]).astype(BF16)
    lane_lo = lax.broadcasted_iota(jnp.int32, (SGU_CHUNK, LANES), 1) < HEAD_DIM
    rows = []
    for c in range(n_chunks):
        cols = []
        for q in range(d_a // LANES):
            vq = vn[c * SGU_CHUNK:(c + 1) * SGU_CHUNK, q * LANES:(q + 1) * LANES]
            cols.append(_dot(w_ref[q], _stack_pair(vq, lane_lo)))
        rows.append(jnp.concatenate(cols, axis=1) + bias_ref[...])
    mixed = jnp.concatenate(rows, axis=0) if n_chunks > 1 else rows[0]
    o_ref[0] = (u * mixed).astype(o_ref.dtype)


def _sgu(p_a, ln_g, ln_b, w_masked, bias_cols, tile, layer):
    bsz, seq, two_da = p_a.shape
    d_a = two_da // 2
    return pl.pallas_call(
        functools.partial(_with_layer_rows(_sgu_kernel, layer, (1, 2)), d_a=d_a, n_chunks=tile // SGU_CHUNK),
        grid=(bsz, seq // tile),
        in_specs=[
            pl.BlockSpec((1, tile, two_da), lambda b, t: (b, t, 0)),
            _layer_slab(ln_g, layer), _layer_slab(ln_b, layer),
            _layer_slab(w_masked, layer), _layer_slab(bias_cols, layer),
        ],
        out_specs=pl.BlockSpec((1, tile, d_a), lambda b, t: (b, t, 0)),
        out_shape=jax.ShapeDtypeStruct((bsz, seq, d_a), BF16),
        compiler_params=_params(("arbitrary", "arbitrary")),
        name="sgu",
    )(p_a, ln_g, ln_b, w_masked, bias_cols)


def _rglru_kernel(p_ref, cw_ref, cb_ref, wbd_ref, bra_ref, bix_ref, lam_ref, o_ref,
                  xtail, hprev, *, d_b, tile):
    t = pl.program_id(1)

    @pl.when(t == 0)
    def _():
        xtail[...] = jnp.zeros_like(xtail)
        hprev[...] = jnp.zeros_like(hprev)

    n_grp = tile // SUBLANES
    sub = lax.broadcasted_iota(jnp.int32, (n_grp, SUBLANES, d_b), 1)

    x_new = p_ref[0, :, 0:d_b]
    yg = p_ref[0, :, d_b:2 * d_b]
    x_ext = jnp.concatenate([xtail[...], x_new], axis=0).reshape(n_grp + 1, SUBLANES, d_b)
    xtail[...] = x_new[tile - SUBLANES:tile, :]
    xr3 = cb_ref[...] + cw_ref[CONV_B - 1:CONV_B, :] * x_ext[1:]
    for k in range(1, CONV_B):
        rot = pltpu.roll(x_ext, k, 1)
        xr3 = xr3 + cw_ref[CONV_B - 1 - k:CONV_B - k, :] * jnp.where(sub >= k, rot[1:], rot[:-1])
    xr = xr3.reshape(tile, d_b)

    ri = _dot(xr.astype(BF16), wbd_ref[...])
    r = _sigmoid(ri[:, 0:d_b] + bra_ref[...])
    i = _sigmoid(ri[:, d_b:2 * d_b] + bix_ref[...])
    log_a = (-LRU_C * r) * _softplus(-lam_ref[...])
    a = jnp.exp(log_a)
    one_minus_a2 = -jnp.tanh(log_a) * (a * a + 1.0)
    bterm = (one_minus_a2 * lax.rsqrt(jnp.maximum(one_minus_a2, F32_MIN_NORMAL))) * (i * xr)

    a3 = a.reshape(n_grp, SUBLANES, d_b)
    b3 = bterm.reshape(n_grp, SUBLANES, d_b)
    shift = 1
    while shift < SUBLANES:
        keep = sub >= shift
        a_s = jnp.where(keep, pltpu.roll(a3, shift, 1), 1.0)
        b_s = jnp.where(keep, pltpu.roll(b3, shift, 1), 0.0)
        b3 = a3 * b_s + b3
        a3 = a3 * a_s
        shift *= 2
    h = hprev[...]
    groups = []
    for g in range(n_grp):
        hg = b3[g] + a3[g] * h
        groups.append(hg)
        h = hg[SUBLANES - 1:SUBLANES, :]
    hprev[...] = h
    o_ref[0] = (jax.nn.gelu(yg) * jnp.concatenate(groups, axis=0)).astype(o_ref.dtype)


def _rglru(p_b, conv_w, conv_b, w_bd, b_ra, b_ix, lam, tile, layer):
    bsz, seq, two_db = p_b.shape
    d_b = two_db // 2
    return pl.pallas_call(
        functools.partial(_with_layer_rows(_rglru_kernel, layer, (2, 4, 5, 6)), d_b=d_b, tile=tile),
        grid=(bsz, seq // tile),
        in_specs=[pl.BlockSpec((1, tile, two_db), lambda b, t: (b, t, 0))]
        + [_layer_slab(a, layer) for a in (conv_w, conv_b, w_bd, b_ra, b_ix, lam)],
        out_specs=pl.BlockSpec((1, tile, d_b), lambda b, t: (b, t, 0)),
        out_shape=jax.ShapeDtypeStruct((bsz, seq, d_b), BF16),
        scratch_shapes=[
            pltpu.VMEM((SUBLANES, d_b), F32),
            pltpu.VMEM((1, d_b), F32),
        ],
        compiler_params=_params(("arbitrary", "arbitrary")),
        name="rglru",
    )(p_b, conv_w, conv_b, w_bd, b_ra, b_ix, lam)


def _stack_pair(x, lane_lo):
    return jnp.concatenate([jnp.where(lane_lo, x, 0.0), jnp.where(lane_lo, 0.0, x)], axis=0)


RWKV_STAGED = ("kt", "kh", "bh", "khl", "bhl", "v", "rtb", "rt", "gl", "gate", "bonus")
RWKV_STAGED_BF16 = 7
RWKV_TRACE_ORDER = "cpcccpcccpccpc"
RWKV_CARRY_DTYPE = BF16


def _rwkv_prep(p_ref, mu_ref, w0_ref, a0_ref, kk_ref, ka_ref, rk_ref, wl_ref, g2_ref, ones_ref, xbuf, dst,
               *, d_c, tile):
    n_chunks = tile // RWKV_CHUNK
    xbuf[SUBLANES:SUBLANES + tile, :] = p_ref[0]
    p = p_ref[0]
    prev = xbuf[pl.ds(SUBLANES - 1, tile), :]
    xbuf[0:SUBLANES, :] = xbuf[tile:tile + SUBLANES, :]
    ps = p + (prev - p) * mu_ref[...]

    r = ps[:, 0:d_c]
    k = ps[:, d_c:2 * d_c]
    v = ps[:, 2 * d_c:3 * d_c]
    xwa = ps[:, 3 * d_c:3 * d_c + LORA_W + LORA_A]
    xg = ps[:, 3 * d_c + LORA_W + LORA_A:]

    lane_l = lax.broadcasted_iota(jnp.int32, xwa.shape, 1)
    lin = jnp.where(lane_l < LORA_W, jnp.tanh(xwa), xwa).astype(BF16)
    wa_lora = _dot(lin, wl_ref[...])
    w = -_softplus(-(w0_ref[...] + wa_lora[:, 0:d_c]), small_values_matter=False) - 0.5
    lw = -jnp.exp(w)
    a = jax.nn.sigmoid(a0_ref[...] + wa_lora[:, d_c:2 * d_c])
    dst["gate"][...] = _dot(jax.nn.sigmoid(xg).astype(BF16), g2_ref[...])

    yield
    ones2 = ones_ref[...]
    kk = k * kk_ref[...]
    kappa = kk * lax.rsqrt(jnp.maximum(_head_sums(kk * kk, ones2, split=False), 1e-24))
    kmod = k * (1.0 + (a - 1.0) * ka_ref[...])
    bvec = kappa * a
    yield
    dst["bonus"][...] = _head_sums(r * kmod * rk_ref[...], ones2, split=False) * v

    ri = lax.broadcasted_iota(jnp.int32, (LANES, 2 * LANES), 0)
    ci = lax.broadcasted_iota(jnp.int32, (LANES, 2 * LANES), 1) & (LANES - 1)
    shift = RWKV_CHUNK.bit_length() - 1
    tril2 = jnp.where((jnp.right_shift(ri, shift) == jnp.right_shift(ci, shift)) & (ci <= ri),
                      1.0, 0.0).astype(BF16)
    lw_hi, lw_lo = _split(lw)
    c_parts = []
    for m in range(tile // LANES):
        rws = slice(m * LANES, (m + 1) * LANES)
        c_parts.append(_dot(tril2, jnp.concatenate([lw_hi[rws], lw_lo[rws]], axis=0)))
    c_incl = jnp.concatenate(c_parts, axis=0) if len(c_parts) > 1 else c_parts[0]
    c_tot = jnp.concatenate(
        [jnp.broadcast_to(c_incl[(c + 1) * RWKV_CHUNK - 1:(c + 1) * RWKV_CHUNK, :], (RWKV_CHUNK, d_c))
         for c in range(n_chunks)], axis=0)

    yield
    e_neg = jnp.exp(-c_incl)
    e_end = jnp.exp(c_tot - c_incl)
    dst["kt"][...] = (kappa * jnp.exp(c_incl - lw)).astype(BF16)
    rt = r * jnp.exp(c_incl)
    dst["rt"][...] = rt
    dst["rtb"][...] = rt.astype(BF16)
    dst["kh"][...] = (kmod * e_neg).astype(BF16)
    dst["bh"][...] = (bvec * e_neg).astype(BF16)
    dst["khl"][...] = (kmod * e_end).astype(BF16)
    dst["bhl"][...] = (bvec * e_end).astype(BF16)
    dst["v"][...] = v.astype(BF16)
    dst["gl"][...] = jnp.exp(c_tot)


def _rwkv_chains(src, rbar_ref, mmat_ref, obar_ref, gbar_ref, gate_ref, bonus_ref, *, d_c, tile):
    n_chunks = tile // RWKV_CHUNK
    n_pairs = d_c // LANES
    gate_ref[0] = src["gate"][...].astype(gate_ref.dtype)
    bonus_ref[0] = src["bonus"][...].astype(bonus_ref.dtype)

    ii = lax.broadcasted_iota(jnp.int32, (LANES, LANES), 0)
    jj = lax.broadcasted_iota(jnp.int32, (LANES, LANES), 1)
    shift = RWKV_CHUNK.bit_length() - 1
    same_head = jnp.right_shift(ii, shift) == jnp.right_shift(jj, shift)
    strict = same_head & (ii > jj)
    incl = same_head & (ii >= jj)
    eye = ii == jj
    lane_lo = lax.broadcasted_iota(jnp.int32, (RWKV_CHUNK, LANES), 1) < HEAD_DIM
    pair_diag = (lax.broadcasted_iota(jnp.int32, (RWKV_CHUNK, LANES), 1) & (HEAD_DIM - 1)) == \
        lax.broadcasted_iota(jnp.int32, (RWKV_CHUNK, LANES), 0)

    chains = [(c, pi) for c in range(n_chunks) for pi in range(n_pairs)]

    def rows_of(ch):
        return slice(ch[0] * RWKV_CHUNK, (ch[0] + 1) * RWKV_CHUNK)

    def lanes_of(ch):
        return slice(ch[1] * LANES, (ch[1] + 1) * LANES)

    def stacked(name, ch):
        return _stack_pair(src[name][rows_of(ch), lanes_of(ch)], lane_lo)

    def doubled(name, ch):
        x = src[name][rows_of(ch), lanes_of(ch)]
        return jnp.concatenate([x, x], axis=0)

    def per_head(x):
        return jnp.where(lane_lo, x[0:RWKV_CHUNK], x[RWKV_CHUNK:2 * RWKV_CHUNK])

    kts_b = [doubled("kt", ch) for ch in chains]
    vbd = [doubled("v", ch) for ch in chains]

    sc = [_dot_nt(jnp.concatenate([kts_b[n], doubled("rtb", ch)], axis=0),
                  jnp.concatenate([stacked("kh", ch), stacked("bh", ch)], axis=0))
          for n, ch in enumerate(chains)]
    a_k = [jnp.where(strict, s[0:LANES, 0:LANES], 0.0).astype(BF16) for s in sc]
    n_b = [jnp.where(strict, s[0:LANES, LANES:2 * LANES], 0.0) for s in sc]
    p_m = [jnp.where(incl, s[LANES:2 * LANES, 0:LANES], 0.0).astype(BF16) for s in sc]
    q_m = [jnp.where(incl, s[LANES:2 * LANES, LANES:2 * LANES], 0.0).astype(BF16) for s in sc]
    yield

    x_m = [jnp.where(eye, 1.0, 0.0) - n for n in n_b]
    pw = [_dot(n.astype(BF16), n.astype(BF16)) for n in n_b]
    yield
    for _ in range(4):
        y = [_dot(jnp.concatenate([x.astype(BF16), p.astype(BF16)], axis=0), p.astype(BF16))
             for x, p in zip(x_m, pw)]
        x_m = [x + yy[0:LANES] for x, yy in zip(x_m, y)]
        pw = [yy[LANES:2 * LANES] for yy in y]
        yield
    t_m = [(x + _dot(x.astype(BF16), p.astype(BF16))).astype(BF16) for x, p in zip(x_m, pw)]
    yield

    akv = [_dot(a, vv) for a, vv in zip(a_k, vbd)]
    yield
    ty_b = [_dot(tm, jnp.concatenate([kb, a.astype(BF16)], axis=1)).astype(BF16)
            for tm, kb, a in zip(t_m, kts_b, akv)]
    yield
    rhs = [jnp.concatenate([jnp.concatenate([jnp.zeros_like(vv), vv], axis=1), tyb], axis=0)
           for vv, tyb in zip(vbd, ty_b)]
    tok = [_dot(jnp.concatenate([p, -q], axis=1), rr) for p, q, rr in zip(p_m, q_m, rhs)]
    chn = [_dot_tn(jnp.concatenate([stacked("khl", ch), -stacked("bhl", ch)], axis=0), rr)
           for ch, rr in zip(chains, rhs)]
    yield

    for n, ch in enumerate(chains):
        rows, lanes = rows_of(ch), lanes_of(ch)
        rbar = src["rt"][rows, lanes] + per_head(tok[n][:, 0:LANES])
        gl_row = src["gl"][ch[0] * RWKV_CHUNK:ch[0] * RWKV_CHUNK + 1, lanes]
        mmat = jnp.where(pair_diag, gl_row, 0.0) + per_head(chn[n][:, 0:LANES])
        rbar_ref[0, rows, lanes] = rbar.astype(rbar_ref.dtype)
        mmat_ref[0, rows, lanes] = mmat.astype(mmat_ref.dtype)
        obar_ref[0, rows, lanes] = per_head(tok[n][:, LANES:2 * LANES]).astype(obar_ref.dtype)
        gbar_ref[0, rows, lanes] = per_head(chn[n][:, LANES:2 * LANES]).astype(gbar_ref.dtype)


def _rwkv_chunk_kernel(p_ref, mu_ref, w0_ref, a0_ref, kk_ref, ka_ref, rk_ref, wl_ref, g2_ref, ones_ref,
                       rbar_ref, mmat_ref, obar_ref, gbar_ref, gate_ref, bonus_ref,
                       xbuf, *staged, d_c, tile, tiles_per_seq):
    s = pl.program_id(0)
    n = len(RWKV_STAGED)
    set_a = dict(zip(RWKV_STAGED, staged[:n]))
    set_b = dict(zip(RWKV_STAGED, staged[n:]))

    @pl.when(s == 0)
    def _():
        for ref in staged[n:]:
            ref[...] = jnp.zeros_like(ref)

    @pl.when(lax.rem(s, tiles_per_seq) == 0)
    def _():
        xbuf[0:SUBLANES, :] = jnp.zeros((SUBLANES, xbuf.shape[-1]), F32)

    def step(dst, src):
        prep = _rwkv_prep(p_ref, mu_ref, w0_ref, a0_ref, kk_ref, ka_ref, rk_ref, wl_ref, g2_ref, ones_ref, xbuf,
                          dst, d_c=d_c, tile=tile)
        chains = _rwkv_chains(src, rbar_ref, mmat_ref, obar_ref, gbar_ref, gate_ref, bonus_ref,
                              d_c=d_c, tile=tile)
        for who in RWKV_TRACE_ORDER:
            next(prep if who == "p" else chains, None)
        for _ in chains:
            pass
        for _ in prep:
            pass

    parity = lax.rem(s, 2)

    @pl.when(parity == 0)
    def _():
        step(set_a, set_b)

    @pl.when(parity == 1)
    def _():
        step(set_b, set_a)


def _rwkv_chunk(p_c, mu, w0, a0, k_k, k_a, r_k, w_lora, g2, ones2, tile, layer):
    bsz, seq, pc = p_c.shape
    d_c = w0.shape[-1]
    n_t = seq // tile
    n_tiles = bsz * n_t

    def staged_tile(s):
        q = jnp.minimum(s, n_tiles - 1)
        return (q // n_t, q % n_t, 0)

    def finished_tile(s):
        q = jnp.maximum(s - 1, 0)
        return (q // n_t, q % n_t, 0)

    tok = pl.BlockSpec((1, tile, d_c), finished_tile)
    shp = lambda dt: jax.ShapeDtypeStruct((bsz, seq, d_c), dt)
    one_set = ([pltpu.VMEM((tile, d_c), BF16)] * RWKV_STAGED_BF16
               + [pltpu.VMEM((tile, d_c), F32)] * (len(RWKV_STAGED) - RWKV_STAGED_BF16))
    return pl.pallas_call(
        functools.partial(_with_layer_rows(_rwkv_chunk_kernel, layer, (1, 2, 3, 4, 5, 6)), d_c=d_c, tile=tile,
                          tiles_per_seq=n_t),
        grid=(n_tiles + 1,),
        in_specs=[pl.BlockSpec((1, tile, pc), staged_tile)]
        + [_layer_slab(a, layer) for a in (mu, w0, a0, k_k, k_a, r_k, w_lora, g2)]
        + [pl.BlockSpec(ones2.shape, lambda s: (0, 0))],
        out_specs=[tok, tok, tok, tok, tok, tok],
        out_shape=[shp(RWKV_CARRY_DTYPE)] * 6,
        scratch_shapes=[pltpu.VMEM((tile + SUBLANES, pc), F32)] + one_set + one_set,
        compiler_params=_params(("arbitrary",)),
        name="rwkv_chunk",
    )(p_c, mu, w0, a0, k_k, k_a, r_k, w_lora, g2, ones2)


def _rwkv_scan_kernel(rbar_ref, mmat_ref, obar_ref, gbar_ref, gate_ref, bonus_ref, lnw_ref, lnb_ref, ones_ref,
                      y_ref, h_s, o_s, *, d_c, tile):
    t = pl.program_id(0)
    bsz = obar_ref.shape[0]
    n_chunks = tile // RWKV_CHUNK
    n_pairs = d_c // LANES

    @pl.when(t == 0)
    def _():
        h_s[...] = jnp.zeros_like(h_s)

    lane_lo = lax.broadcasted_iota(jnp.int32, (RWKV_CHUNK, LANES), 1) < HEAD_DIM
    chains = [(b, pi) for b in range(bsz) for pi in range(n_pairs)]

    ones2 = ones_ref[...]
    inv = 1.0 / HEAD_DIM

    def finish(c):
        rows = slice(c * RWKV_CHUNK, (c + 1) * RWKV_CHUNK)
        o = o_s[:, rows, :].reshape(bsz * RWKV_CHUNK, d_c)
        mean = _head_sums(o, ones2) * inv
        dlt = o - mean
        var = _head_sums(dlt * dlt, ones2) * inv
        gn = (dlt * lax.rsqrt(var + GN_EPS) * lnw_ref[...] + lnb_ref[...]).reshape(bsz, RWKV_CHUNK, d_c)
        y_ref[:, rows, :] = ((gn + bonus_ref[:, rows, :]) * gate_ref[:, rows, :]).astype(y_ref.dtype)

    for c in range(n_chunks):
        rows = slice(c * RWKV_CHUNK, (c + 1) * RWKV_CHUNK)
        outs = []
        for n, (b, pi) in enumerate(chains):
            lanes = slice(pi * LANES, (pi + 1) * LANES)
            hbd = _stack_pair(h_s[n], lane_lo).astype(BF16)
            lhs = jnp.concatenate([rbar_ref[b, rows, lanes], mmat_ref[b, rows, lanes]], axis=0)
            outs.append(_dot(lhs, hbd))
        if c > 0:
            finish(c - 1)
        for n, (b, pi) in enumerate(chains):
            lanes = slice(pi * LANES, (pi + 1) * LANES)
            o_s[b, rows, lanes] = outs[n][0:RWKV_CHUNK] + obar_ref[b, rows, lanes]
            h_s[n] = outs[n][RWKV_CHUNK:2 * RWKV_CHUNK] + gbar_ref[b, rows, lanes]
    finish(n_chunks - 1)


def _rwkv_scan(rbar, mmat, obar, gbar, gate, bonus, ln_w, ln_b, ones2, tile, layer):
    bsz, seq, d_c = obar.shape
    tok = pl.BlockSpec((bsz, tile, d_c), lambda t: (0, t, 0))
    return pl.pallas_call(
        functools.partial(_with_layer_rows(_rwkv_scan_kernel, layer, (6, 7)), d_c=d_c, tile=tile),
        grid=(seq // tile,),
        in_specs=[tok, tok, tok, tok, tok, tok, _layer_slab(ln_w, layer), _layer_slab(ln_b, layer),
                  pl.BlockSpec(ones2.shape, lambda t: (0, 0))],
        out_specs=tok,
        out_shape=jax.ShapeDtypeStruct((bsz, seq, d_c), BF16),
        scratch_shapes=[
            pltpu.VMEM((bsz * (d_c // LANES), RWKV_CHUNK, LANES), F32),
            pltpu.VMEM((bsz, tile, d_c), F32),
        ],
        compiler_params=_params(("arbitrary",)),
        name="rwkv_scan",
    )(rbar, mmat, obar, gbar, gate, bonus, ln_w, ln_b, ones2)


def _mixer_out(x_ref, ya_ref, yb_ref, yc_ref, mod_ref, wo_ref, g_ref, dst, *, d):
    y = _dot(jnp.concatenate([ya_ref[0], yb_ref[0], yc_ref[0]], axis=1), wo_ref[...])
    yield
    x1 = x_ref[0] + mod_ref[0, :, 2 * d:3 * d] * y
    dst["x1"][...] = x1
    n = x1 * lax.rsqrt(jnp.mean(x1 * x1, axis=-1, keepdims=True) + EPS)
    h2 = (n * g_ref[...]) * (1.0 + mod_ref[0, :, 4 * d:5 * d]) + mod_ref[0, :, 3 * d:4 * d]
    dst["h2"][...] = h2.astype(BF16)


def _conv_glu_ffn(src, mod_ref, wup_ref, cw_ref, cb_ref, wd_ref, gf_ref, o_ref, gbuf, gtail,
                  *, d, tile, d_ff, splits, final_norm):
    h2 = src["h2"][...]
    mids = []
    for lo, hi in splits:
        gbuf[0:SUBLANES, lo:hi] = gtail[:, lo:hi]
        gbuf[SUBLANES:SUBLANES + tile, lo:hi] = _dot(h2, wup_ref[:, lo:hi])
        yield
        val = _dot(h2, wup_ref[:, d_ff + lo:d_ff + hi])
        gc = cb_ref[:, lo:hi] + cw_ref[0:1, lo:hi] * gbuf[pl.ds(SUBLANES - (CONV_FF - 1), tile), lo:hi]
        for q in range(1, CONV_FF):
            gc = gc + cw_ref[q:q + 1, lo:hi] * gbuf[pl.ds(SUBLANES - (CONV_FF - 1) + q, tile), lo:hi]
        gtail[:, lo:hi] = gbuf[tile:tile + SUBLANES, lo:hi]
        mids.append((gc * jax.nn.sigmoid(gc) * val).astype(BF16))
        yield
    ffn = _dot(mids[0], wd_ref[splits[0][0]:splits[0][1], :])
    for mid, (lo, hi) in zip(mids[1:], splits[1:]):
        ffn = ffn + _dot(mid, wd_ref[lo:hi, :])
    x2 = src["x1"][...] + mod_ref[0, :, 5 * d:6 * d] * ffn
    if final_norm:
        x2 = x2 * lax.rsqrt(jnp.mean(x2 * x2, axis=-1, keepdims=True) + EPS) * gf_ref[...]
    o_ref[0] = x2


def _out_ffn_kernel(x_ref, ya_ref, yb_ref, yc_ref, mod_in_ref, mod_out_ref, wo_ref, g_ref,
                    wup_ref, cw_ref, cb_ref, wd_ref, gf_ref, o_ref,
                    gbuf, gtail, x1_a, h2_a, x1_b, h2_b, *, d, tile, d_ff, splits, tiles_per_seq, final_norm):
    s = pl.program_id(0)
    set_a = {"x1": x1_a, "h2": h2_a}
    set_b = {"x1": x1_b, "h2": h2_b}

    @pl.when(s == 0)
    def _():
        x1_b[...] = jnp.zeros_like(x1_b)
        h2_b[...] = jnp.zeros_like(h2_b)

    @pl.when(lax.rem(jnp.maximum(s - 1, 0), tiles_per_seq) == 0)
    def _():
        gtail[...] = jnp.zeros_like(gtail)

    def step(dst, src):
        head = _mixer_out(x_ref, ya_ref, yb_ref, yc_ref, mod_in_ref, wo_ref, g_ref, dst, d=d)
        ffn = _conv_glu_ffn(src, mod_out_ref, wup_ref, cw_ref, cb_ref, wd_ref, gf_ref, o_ref, gbuf, gtail,
                            d=d, tile=tile, d_ff=d_ff, splits=splits, final_norm=final_norm)
        for who in OUT_FFN_TRACE_ORDER:
            next(head if who == "h" else ffn, None)
        for _ in ffn:
            pass
        for _ in head:
            pass

    parity = lax.rem(s, 2)

    @pl.when(parity == 0)
    def _():
        step(set_a, set_b)

    @pl.when(parity == 1)
    def _():
        step(set_b, set_a)


def _column_splits(width):
    half = (width // (2 * MXU_TILE)) * MXU_TILE or width // 2
    return ((0, width - half), (width - half, width))


def _out_ffn(x, y_a, y_b, y_c, mod, w_out, g_ffn, w_up, conv_w, conv_b, w_down, g_final, tile, final_norm,
             layer):
    bsz, seq, d = x.shape
    d_ff = w_down.shape[1]
    n_t = seq // tile
    n_tiles = bsz * n_t
    resident = lambda a: _layer_slab(a, layer, single_buffer=True)

    def staged(s):
        q = jnp.minimum(s, n_tiles - 1)
        return q // n_t, q % n_t

    def finished(s):
        q = jnp.maximum(s - 1, 0)
        return q // n_t, q % n_t

    tok_in = lambda w: pl.BlockSpec((1, tile, w), lambda s: (*staged(s), 0))
    return pl.pallas_call(
        functools.partial(_with_layer_rows(_out_ffn_kernel, layer, (7, 10)), d=d, tile=tile, d_ff=d_ff,
                          splits=_column_splits(d_ff),
                          tiles_per_seq=n_t, final_norm=final_norm),
        grid=(n_tiles + 1,),
        in_specs=[
            tok_in(d), tok_in(y_a.shape[-1]), tok_in(y_b.shape[-1]), tok_in(y_c.shape[-1]),
            _mod_rows(mod, layer, lambda s: staged(s)[0]), _mod_rows(mod, layer, lambda s: finished(s)[0]),
            resident(w_out),
            _layer_slab(g_ffn, layer),
            resident(w_up),
            _layer_slab(conv_w, layer),
            _layer_slab(conv_b, layer),
            resident(w_down),
            pl.BlockSpec((1, d), lambda s: (0, 0)),
        ],
        out_specs=pl.BlockSpec((1, tile, d), lambda s: (*finished(s), 0)),
        out_shape=jax.ShapeDtypeStruct((bsz, seq, d), F32),
        scratch_shapes=[
            pltpu.VMEM((tile + SUBLANES, d_ff), F32),
            pltpu.VMEM((SUBLANES, d_ff), F32),
            pltpu.VMEM((tile, d), F32), pltpu.VMEM((tile, d), BF16),
            pltpu.VMEM((tile, d), F32), pltpu.VMEM((tile, d), BF16),
        ],
        compiler_params=_params(("arbitrary",)),
        name="out_ffn",
    )(x, y_a, y_b, y_c, mod, mod, w_out, g_ffn, w_up, conv_w, conv_b, w_down, g_final)


def _block_diag(w):
    h, n, _ = w.shape
    eye = jnp.eye(h, dtype=w.dtype)
    return (eye[:, None, :, None] * w[:, :, None, :]).reshape(h * n, h * n)


def _pick_tile(seq, want):
    tile = min(seq, want)
    while seq % tile:
        tile //= 2
    return tile


def kernel(x, c, w_mod, b_mod, norm_mix, w_in, w_out, sgu_ln_g, sgu_ln_b, sgu_w, sgu_b, lru_conv_w, lru_conv_b, lru_w_a, lru_b_a, lru_w_x, lru_b_x, lru_lambda, rwkv_mu, rwkv_w0, rwkv_w2, rwkv_a0, rwkv_a2, rwkv_g2, rwkv_k_k, rwkv_k_a, rwkv_r_k, rwkv_ln_w, rwkv_ln_b, norm_ffn, ffn_w_up, ffn_conv_w, ffn_conv_b, ffn_w_down, norm_final):
    bsz, seq, d = x.shape
    depth = w_in.shape[0]
    d_a = sgu_ln_g.shape[-1]
    d_b = lru_conv_b.shape[-1]
    d_c = rwkv_w0.shape[-1]
    p_a, p_b = 2 * d_a, 2 * d_b

    t_proj = _pick_tile(seq, 1024)
    t_sgu = _pick_tile(seq, 2048)
    t_lru = _pick_tile(seq, 512)
    t_chunk = _pick_tile(seq, 256)
    t_scan = _pick_tile(seq, 256)
    t_ffn = _pick_tile(seq, 512)

    mod = _modulation(c, w_mod, b_mod).reshape(depth, bsz, 1, N_MOD * d)
    ones_pair = _block_diag(jnp.ones((LANES // HEAD_DIM, HEAD_DIM, HEAD_DIM), BF16))
    ones2 = jnp.concatenate([ones_pair, ones_pair], axis=0)
    tril = jnp.tril(jnp.ones((SGU_CHUNK, SGU_CHUNK), dtype=bool))
    w_in_b = w_in.astype(BF16)
    w_out_b = w_out.astype(BF16)
    w_up_b = ffn_w_up.astype(BF16)
    w_down_b = ffn_w_down.astype(BF16)
    heads_per_group = LANES // HEAD_DIM
    sgu_wm = jnp.where(tril, sgu_w, 0.0).astype(BF16)
    sgu_wm = sgu_wm.reshape(depth, -1, heads_per_group, SGU_CHUNK, SGU_CHUNK)
    sgu_wm = jnp.swapaxes(sgu_wm, 2, 3).reshape(depth, -1, SGU_CHUNK, heads_per_group * SGU_CHUNK)
    sgu_bias = jnp.repeat(jnp.swapaxes(sgu_b, 1, 2), HEAD_DIM, axis=2)
    block_diag_l = jax.vmap(_block_diag)
    lru_w_bd = jnp.concatenate([block_diag_l(lru_w_a), block_diag_l(lru_w_x)], axis=2).astype(BF16)
    zeros_l = jnp.zeros((depth, LORA_W, d_c), F32)
    w_lora = jnp.concatenate([
        jnp.concatenate([rwkv_w2, zeros_l], axis=2),
        jnp.concatenate([zeros_l, rwkv_a2], axis=2)], axis=1).astype(BF16)
    g2_b = rwkv_g2.astype(BF16)

    r_k = rwkv_r_k.reshape(depth, d_c)

    for l in range(depth):
        pa, pb, pc = _in_proj(x, mod, norm_mix, w_in_b, (p_a, p_b, w_in.shape[-1] - p_a - p_b), t_proj, l)
        y_a = _sgu(pa, sgu_ln_g, sgu_ln_b, sgu_wm, sgu_bias, t_sgu, l)
        y_b = _rglru(pb, lru_conv_w, lru_conv_b, lru_w_bd, lru_b_a, lru_b_x, lru_lambda, t_lru, l)
        rbar, mmat, obar, gbar, gate, bonus = _rwkv_chunk(
            pc, rwkv_mu, rwkv_w0, rwkv_a0, rwkv_k_k, rwkv_k_a, r_k, w_lora, g2_b, ones2, t_chunk, l)
        y_c = _rwkv_scan(rbar, mmat, obar, gbar, gate, bonus, rwkv_ln_w, rwkv_ln_b, ones2, t_scan, l)
        x = _out_ffn(x, y_a, y_b, y_c, mod, w_out_b, norm_ffn, w_up_b, ffn_conv_w, ffn_conv_b,
                     w_down_b, norm_final.reshape(1, -1), t_ffn, l == depth - 1, l)
    return x
```

```python
import functools

import jax
import jax.numpy as jnp
from jax import lax
from jax.experimental import pallas as pl
from jax.experimental.pallas import tpu as pltpu

HEAD_DIM = 64
SGU_CHUNK = 128
CONV_B = 4
LRU_C = 8.0
LORA_W = 64
LORA_A = 64
LORA_G = 128
CONV_FF = 3
N_MOD = 6
EPS = 1e-6
LN_EPS = 1e-5
GN_EPS = 64e-5

RWKV_CHUNK = 64
LANES = 128
MXU_TILE = 256
IN_PROJ_ROW_BLOCK = 256
OUT_FFN_TRACE_ORDER = "hfhfff"
SUBLANES = 8
VMEM_LIMIT = 56 * 1024 * 1024
F32_MIN_NORMAL = 1.1754944e-38

BF16 = jnp.bfloat16
F32 = jnp.float32


def _dot(a, b):
    return jnp.dot(a, b, preferred_element_type=F32)


def _dot_nt(a, b):
    return lax.dot_general(a, b, (((1,), (1,)), ((), ())), preferred_element_type=F32)


def _dot_tn(a, b):
    return lax.dot_general(a, b, (((0,), (0,)), ((), ())), preferred_element_type=F32)


def _split(x):
    hi = x.astype(BF16)
    lo = (x - hi.astype(F32)).astype(BF16)
    return hi, lo


def _head_sums(x, ones2, split=True):
    outs = []
    for pi in range(x.shape[-1] // LANES):
        xp = x[:, pi * LANES:(pi + 1) * LANES]
        if split:
            hi, lo = _split(xp)
            outs.append(_dot(jnp.concatenate([hi, lo], axis=1), ones2))
        else:
            outs.append(_dot(xp.astype(BF16), ones2[0:LANES]))
    return jnp.concatenate(outs, axis=1)


def _sigmoid(z):
    return 0.5 * jnp.tanh(0.5 * z) + 0.5


def _softplus(z, small_values_matter=True):
    t = jnp.exp(-jnp.abs(z))
    return jnp.maximum(z, 0.0) + (jnp.log1p(t) if small_values_matter else jnp.log(1.0 + t))


def _params(sem):
    return pltpu.CompilerParams(dimension_semantics=sem, vmem_limit_bytes=VMEM_LIMIT)


def _layer_slab(arr, layer, single_buffer=False):
    if arr.ndim == 2:
        return pl.BlockSpec(arr.shape, lambda *_: (0, 0))
    zeros = (0,) * (arr.ndim - 1)
    mode = {"pipeline_mode": pl.Buffered(1)} if single_buffer else {}
    return pl.BlockSpec((None,) + arr.shape[1:], lambda *_: (layer,) + zeros, **mode)


def _with_layer_rows(kernel_fn, layer, positions):
    def narrowed(*refs, **kwargs):
        refs = list(refs)
        for i in positions:
            refs[i] = refs[i].at[pl.ds(layer, 1)]
        return kernel_fn(*refs, **kwargs)
    return narrowed


def _mod_rows(mod, layer, batch_of):
    return pl.BlockSpec((None, 1, 1, mod.shape[-1]), lambda *g: (layer, batch_of(*g), 0, 0))


def _mod_kernel(c_ref, w_ref, b_ref, o_ref):
    c = c_ref[...]
    ca = c * jax.nn.sigmoid(c)
    w = w_ref[0]
    chi, clo = _split(ca)
    whi, wlo = _split(w)
    acc = _dot(chi, whi) + _dot(chi, wlo) + _dot(clo, whi)
    o_ref[0] = acc + b_ref[0]


def _modulation(c, w_mod, b_mod):
    depth, d, nd = w_mod.shape
    bsz = c.shape[0]
    nblk = nd // d
    return pl.pallas_call(
        _mod_kernel,
        grid=(depth, nblk),
        in_specs=[
            pl.BlockSpec((bsz, d), lambda l, j: (0, 0)),
            pl.BlockSpec((1, d, d), lambda l, j: (l, 0, j)),
            pl.BlockSpec((1, 1, d), lambda l, j: (l, 0, j)),
        ],
        out_specs=pl.BlockSpec((1, bsz, d), lambda l, j: (l, 0, j)),
        out_shape=jax.ShapeDtypeStruct((depth, bsz, nd), F32),
        compiler_params=_params(("arbitrary", "arbitrary")),
        name="adaln_mod",
    )(c, w_mod, b_mod.reshape(depth, 1, nd))


def _in_proj_kernel(x_ref, mod_ref, g_ref, mu_ref, w_ref, pa_ref, pb_ref, pc_ref, last_row,
                    *, d, tile, row_block):
    na, nb, nc = pa_ref.shape[-1], pb_ref.shape[-1], pc_ref.shape[-1]

    @pl.when(pl.program_id(1) == 0)
    def _():
        last_row[...] = jnp.zeros_like(last_row)

    scale = 1.0 + mod_ref[0, :, d:2 * d]
    shift = mod_ref[0, :, 0:d]
    first_row = lax.broadcasted_iota(jnp.int32, (row_block, nc), 0) == 0
    carry = last_row[...]
    for r in range(tile // row_block):
        rows = slice(r * row_block, (r + 1) * row_block)
        x = x_ref[0, rows, :]
        y = x * lax.rsqrt(jnp.mean(x * x, axis=-1, keepdims=True) + EPS)
        h = (y * g_ref[...]) * scale + shift
        p = _dot(h.astype(BF16), w_ref[...])
        pa_ref[0, rows, :] = p[:, 0:na]
        pb_ref[0, rows, :] = p[:, na:na + nb]
        pc = p[:, na + nb:]
        prev = jnp.where(first_row, carry, pltpu.roll(pc, 1, 0))
        pc_ref[0, rows, :] = pc + (prev - pc) * mu_ref[...]
        carry = pc[row_block - 1:row_block, :]
    last_row[...] = carry


def _in_proj(x, mod, g, mu, w_in, widths, tile, layer):
    bsz, seq, d = x.shape
    na, nb, nc = widths
    return pl.pallas_call(
        functools.partial(_with_layer_rows(_in_proj_kernel, layer, (2, 3)), d=d, tile=tile,
                          row_block=min(tile, IN_PROJ_ROW_BLOCK)),
        grid=(bsz, seq // tile),
        in_specs=[
            pl.BlockSpec((1, tile, d), lambda b, t: (b, t, 0)),
            _mod_rows(mod, layer, lambda b, t: b),
            _layer_slab(g, layer),
            _layer_slab(mu, layer),
            _layer_slab(w_in, layer, single_buffer=True),
        ],
        out_specs=[
            pl.BlockSpec((1, tile, na), lambda b, t: (b, t, 0)),
            pl.BlockSpec((1, tile, nb), lambda b, t: (b, t, 0)),
            pl.BlockSpec((1, tile, nc), lambda b, t: (b, t, 0)),
        ],
        out_shape=[
            jax.ShapeDtypeStruct((bsz, seq, na), F32),
            jax.ShapeDtypeStruct((bsz, seq, nb), F32),
            jax.ShapeDtypeStruct((bsz, seq, nc), F32),
        ],
        scratch_shapes=[pltpu.VMEM((1, nc), F32)],
        compiler_params=_params(("arbitrary", "arbitrary")),
        name="in_proj",
    )(x, mod, g, mu, w_in)


def _sgu_kernel(p_ref, lng_ref, lnb_ref, w_ref, bias_ref, o_ref, *, d_a, n_chunks):
    z = jax.nn.gelu(p_ref[0])
    u = z[:, :d_a]
    v = z[:, d_a:]
    mu = jnp.mean(v, axis=-1, keepdims=True)
    var = jnp.mean(jnp.square(v - mu), axis=-1, keepdims=True)
    vn = ((v - mu) * lax.rsqrt(var + LN_EPS) * lng_ref[...] + lnb_ref[...]).astype(BF16)
    lane_lo = lax.broadcasted_iota(jnp.int32, (SGU_CHUNK, LANES), 1) < HEAD_DIM
    rows = []
    for c in range(n_chunks):
        cols = []
        for q in range(d_a // LANES):
            vq = vn[c * SGU_CHUNK:(c + 1) * SGU_CHUNK, q * LANES:(q + 1) * LANES]
            cols.append(_dot(w_ref[q], _stack_pair(vq, lane_lo)))
        rows.append(jnp.concatenate(cols, axis=1) + bias_ref[...])
    mixed = jnp.concatenate(rows, axis=0) if n_chunks > 1 else rows[0]
    o_ref[0] = (u * mixed).astype(o_ref.dtype)


def _sgu(p_a, ln_g, ln_b, w_masked, bias_cols, tile, layer):
    bsz, seq, two_da = p_a.shape
    d_a = two_da // 2
    return pl.pallas_call(
        functools.partial(_with_layer_rows(_sgu_kernel, layer, (1, 2)), d_a=d_a, n_chunks=tile // SGU_CHUNK),
        grid=(bsz, seq // tile),
        in_specs=[
            pl.BlockSpec((1, tile, two_da), lambda b, t: (b, t, 0)),
            _layer_slab(ln_g, layer), _layer_slab(ln_b, layer),
            _layer_slab(w_masked, layer), _layer_slab(bias_cols, layer),
        ],
        out_specs=pl.BlockSpec((1, tile, d_a), lambda b, t: (b, t, 0)),
        out_shape=jax.ShapeDtypeStruct((bsz, seq, d_a), BF16),
        compiler_params=_params(("arbitrary", "arbitrary")),
        name="sgu",
    )(p_a, ln_g, ln_b, w_masked, bias_cols)


def _rglru_kernel(p_ref, cw_ref, cb_ref, wbd_ref, bra_ref, bix_ref, lam_ref, o_ref,
                  xtail, hprev, *, d_b, tile):
    t = pl.program_id(1)

    @pl.when(t == 0)
    def _():
        xtail[...] = jnp.zeros_like(xtail)
        hprev[...] = jnp.zeros_like(hprev)

    n_grp = tile // SUBLANES
    sub = lax.broadcasted_iota(jnp.int32, (n_grp, SUBLANES, d_b), 1)

    x_new = p_ref[0, :, 0:d_b]
    yg = p_ref[0, :, d_b:2 * d_b]
    x_ext = jnp.concatenate([xtail[...], x_new], axis=0).reshape(n_grp + 1, SUBLANES, d_b)
    xtail[...] = x_new[tile - SUBLANES:tile, :]
    xr3 = cb_ref[...] + cw_ref[CONV_B - 1:CONV_B, :] * x_ext[1:]
    for k in range(1, CONV_B):
        rot = pltpu.roll(x_ext, k, 1)
        xr3 = xr3 + cw_ref[CONV_B - 1 - k:CONV_B - k, :] * jnp.where(sub >= k, rot[1:], rot[:-1])
    xr = xr3.reshape(tile, d_b)

    ri = _dot(xr.astype(BF16), wbd_ref[...])
    r = _sigmoid(ri[:, 0:d_b] + bra_ref[...])
    i = _sigmoid(ri[:, d_b:2 * d_b] + bix_ref[...])
    log_a = (-LRU_C * r) * _softplus(-lam_ref[...])
    a = jnp.exp(log_a)
    one_minus_a2 = -jnp.tanh(log_a) * (a * a + 1.0)
    bterm = (one_minus_a2 * lax.rsqrt(jnp.maximum(one_minus_a2, F32_MIN_NORMAL))) * (i * xr)

    a3 = a.reshape(n_grp, SUBLANES, d_b)
    b3 = bterm.reshape(n_grp, SUBLANES, d_b)
    shift = 1
    while shift < SUBLANES:
        keep = sub >= shift
        a_s = jnp.where(keep, pltpu.roll(a3, shift, 1), 1.0)
        b_s = jnp.where(keep, pltpu.roll(b3, shift, 1), 0.0)
        b3 = a3 * b_s + b3
        a3 = a3 * a_s
        shift *= 2
    h = hprev[...]
    groups = []
    for g in range(n_grp):
        hg = b3[g] + a3[g] * h
        groups.append(hg)
        h = hg[SUBLANES - 1:SUBLANES, :]
    hprev[...] = h
    o_ref[0] = (jax.nn.gelu(yg) * jnp.concatenate(groups, axis=0)).astype(o_ref.dtype)


def _rglru(p_b, conv_w, conv_b, w_bd, b_ra, b_ix, lam, tile, layer):
    bsz, seq, two_db = p_b.shape
    d_b = two_db // 2
    return pl.pallas_call(
        functools.partial(_with_layer_rows(_rglru_kernel, layer, (2, 4, 5, 6)), d_b=d_b, tile=tile),
        grid=(bsz, seq // tile),
        in_specs=[pl.BlockSpec((1, tile, two_db), lambda b, t: (b, t, 0))]
        + [_layer_slab(a, layer) for a in (conv_w, conv_b, w_bd, b_ra, b_ix, lam)],
        out_specs=pl.BlockSpec((1, tile, d_b), lambda b, t: (b, t, 0)),
        out_shape=jax.ShapeDtypeStruct((bsz, seq, d_b), BF16),
        scratch_shapes=[
            pltpu.VMEM((SUBLANES, d_b), F32),
            pltpu.VMEM((1, d_b), F32),
        ],
        compiler_params=_params(("arbitrary", "arbitrary")),
        name="rglru",
    )(p_b, conv_w, conv_b, w_bd, b_ra, b_ix, lam)


def _stack_pair(x, lane_lo):
    return jnp.concatenate([jnp.where(lane_lo, x, 0.0), jnp.where(lane_lo, 0.0, x)], axis=0)


RWKV_STAGED = ("kt", "kh", "bh", "khl", "bhl", "v", "rtb", "rt", "gl", "gate", "bonus")
RWKV_STAGED_BF16 = 7
RWKV_TRACE_ORDER = "ccpcccpcccpccp"
RWKV_CARRY_DTYPE = BF16


def _rwkv_prep(p_ref, w0_ref, a0_ref, kk_ref, ka_ref, rk_ref, wl_ref, g2_ref, ones_ref, dst, *, d_c, tile):
    n_chunks = tile // RWKV_CHUNK
    ps = p_ref[0]

    r = ps[:, 0:d_c]
    k = ps[:, d_c:2 * d_c]
    v = ps[:, 2 * d_c:3 * d_c]
    xwa = ps[:, 3 * d_c:3 * d_c + LORA_W + LORA_A]
    xg = ps[:, 3 * d_c + LORA_W + LORA_A:]

    lane_l = lax.broadcasted_iota(jnp.int32, xwa.shape, 1)
    lin = jnp.where(lane_l < LORA_W, jnp.tanh(xwa), xwa).astype(BF16)
    wa_lora = _dot(lin, wl_ref[...])
    w = -_softplus(-(w0_ref[...] + wa_lora[:, 0:d_c]), small_values_matter=False) - 0.5
    lw = -jnp.exp(w)
    a = jax.nn.sigmoid(a0_ref[...] + wa_lora[:, d_c:2 * d_c])
    dst["gate"][...] = _dot(jax.nn.sigmoid(xg).astype(BF16), g2_ref[...])

    yield
    ones2 = ones_ref[...]
    kk = k * kk_ref[...]
    kappa = kk * lax.rsqrt(jnp.maximum(_head_sums(kk * kk, ones2, split=False), 1e-24))
    kmod = k * (1.0 + (a - 1.0) * ka_ref[...])
    bvec = kappa * a
    yield
    dst["bonus"][...] = _head_sums(r * kmod * rk_ref[...], ones2, split=False) * v

    ri = lax.broadcasted_iota(jnp.int32, (LANES, 2 * LANES), 0)
    ci = lax.broadcasted_iota(jnp.int32, (LANES, 2 * LANES), 1) & (LANES - 1)
    shift = RWKV_CHUNK.bit_length() - 1
    tril2 = jnp.where((jnp.right_shift(ri, shift) == jnp.right_shift(ci, shift)) & (ci <= ri),
                      1.0, 0.0).astype(BF16)
    lw_hi, lw_lo = _split(lw)
    c_parts = []
    for m in range(tile // LANES):
        rws = slice(m * LANES, (m + 1) * LANES)
        c_parts.append(_dot(tril2, jnp.concatenate([lw_hi[rws], lw_lo[rws]], axis=0)))
    c_incl = jnp.concatenate(c_parts, axis=0) if len(c_parts) > 1 else c_parts[0]
    c_tot = jnp.concatenate(
        [jnp.broadcast_to(c_incl[(c + 1) * RWKV_CHUNK - 1:(c + 1) * RWKV_CHUNK, :], (RWKV_CHUNK, d_c))
         for c in range(n_chunks)], axis=0)

    yield
    e_neg = jnp.exp(-c_incl)
    e_end = jnp.exp(c_tot - c_incl)
    dst["kt"][...] = (kappa * jnp.exp(c_incl - lw)).astype(BF16)
    rt = r * jnp.exp(c_incl)
    dst["rt"][...] = rt
    dst["rtb"][...] = rt.astype(BF16)
    dst["kh"][...] = (kmod * e_neg).astype(BF16)
    dst["bh"][...] = (bvec * e_neg).astype(BF16)
    dst["khl"][...] = (kmod * e_end).astype(BF16)
    dst["bhl"][...] = (bvec * e_end).astype(BF16)
    dst["v"][...] = v.astype(BF16)
    dst["gl"][...] = jnp.exp(c_tot)


def _rwkv_chains(src, rbar_ref, mmat_ref, obar_ref, gbar_ref, gate_ref, bonus_ref, *, d_c, tile):
    n_chunks = tile // RWKV_CHUNK
    n_pairs = d_c // LANES
    gate_ref[0] = src["gate"][...].astype(gate_ref.dtype)
    bonus_ref[0] = src["bonus"][...].astype(bonus_ref.dtype)

    ii = lax.broadcasted_iota(jnp.int32, (LANES, LANES), 0)
    jj = lax.broadcasted_iota(jnp.int32, (LANES, LANES), 1)
    shift = RWKV_CHUNK.bit_length() - 1
    same_head = jnp.right_shift(ii, shift) == jnp.right_shift(jj, shift)
    strict = same_head & (ii > jj)
    incl = same_head & (ii >= jj)
    eye = ii == jj
    lane_lo = lax.broadcasted_iota(jnp.int32, (RWKV_CHUNK, LANES), 1) < HEAD_DIM
    pair_diag = (lax.broadcasted_iota(jnp.int32, (RWKV_CHUNK, LANES), 1) & (HEAD_DIM - 1)) == \
        lax.broadcasted_iota(jnp.int32, (RWKV_CHUNK, LANES), 0)

    chains = [(c, pi) for c in range(n_chunks) for pi in range(n_pairs)]

    def rows_of(ch):
        return slice(ch[0] * RWKV_CHUNK, (ch[0] + 1) * RWKV_CHUNK)

    def lanes_of(ch):
        return slice(ch[1] * LANES, (ch[1] + 1) * LANES)

    def stacked(name, ch):
        return _stack_pair(src[name][rows_of(ch), lanes_of(ch)], lane_lo)

    def doubled(name, ch):
        x = src[name][rows_of(ch), lanes_of(ch)]
        return jnp.concatenate([x, x], axis=0)

    def per_head(x):
        return jnp.where(lane_lo, x[0:RWKV_CHUNK], x[RWKV_CHUNK:2 * RWKV_CHUNK])

    kts_b = [doubled("kt", ch) for ch in chains]
    vbd = [doubled("v", ch) for ch in chains]

    sc = [_dot_nt(jnp.concatenate([kts_b[n], doubled("rtb", ch)], axis=0),
                  jnp.concatenate([stacked("kh", ch), stacked("bh", ch)], axis=0))
          for n, ch in enumerate(chains)]
    a_k = [jnp.where(strict, s[0:LANES, 0:LANES], 0.0).astype(BF16) for s in sc]
    n_b = [jnp.where(strict, s[0:LANES, LANES:2 * LANES], 0.0) for s in sc]
    p_m = [jnp.where(incl, s[LANES:2 * LANES, 0:LANES], 0.0).astype(BF16) for s in sc]
    q_m = [jnp.where(incl, s[LANES:2 * LANES, LANES:2 * LANES], 0.0).astype(BF16) for s in sc]
    yield

    x_m = [jnp.where(eye, 1.0, 0.0) - n for n in n_b]
    pw = [_dot(n.astype(BF16), n.astype(BF16)) for n in n_b]
    yield
    for _ in range(4):
        y = [_dot(jnp.concatenate([x.astype(BF16), p.astype(BF16)], axis=0), p.astype(BF16))
             for x, p in zip(x_m, pw)]
        x_m = [x + yy[0:LANES] for x, yy in zip(x_m, y)]
        pw = [yy[LANES:2 * LANES] for yy in y]
        yield
    t_m = [(x + _dot(x.astype(BF16), p.astype(BF16))).astype(BF16) for x, p in zip(x_m, pw)]
    yield

    akv = [_dot(a, vv) for a, vv in zip(a_k, vbd)]
    yield
    ty_b = [_dot(tm, jnp.concatenate([kb, a.astype(BF16)], axis=1)).astype(BF16)
            for tm, kb, a in zip(t_m, kts_b, akv)]
    yield
    rhs = [jnp.concatenate([jnp.concatenate([jnp.zeros_like(vv), vv], axis=1), tyb], axis=0)
           for vv, tyb in zip(vbd, ty_b)]
    tok = [_dot(jnp.concatenate([p, -q], axis=1), rr) for p, q, rr in zip(p_m, q_m, rhs)]
    chn = [_dot_tn(jnp.concatenate([stacked("khl", ch), -stacked("bhl", ch)], axis=0), rr)
           for ch, rr in zip(chains, rhs)]
    yield

    for n, ch in enumerate(chains):
        rows, lanes = rows_of(ch), lanes_of(ch)
        rbar = src["rt"][rows, lanes] + per_head(tok[n][:, 0:LANES])
        gl_row = src["gl"][ch[0] * RWKV_CHUNK:ch[0] * RWKV_CHUNK + 1, lanes]
        mmat = jnp.where(pair_diag, gl_row, 0.0) + per_head(chn[n][:, 0:LANES])
        rbar_ref[0, rows, lanes] = rbar.astype(rbar_ref.dtype)
        mmat_ref[0, rows, lanes] = mmat.astype(mmat_ref.dtype)
        obar_ref[0, rows, lanes] = per_head(tok[n][:, LANES:2 * LANES]).astype(obar_ref.dtype)
        gbar_ref[0, rows, lanes] = per_head(chn[n][:, LANES:2 * LANES]).astype(gbar_ref.dtype)


def _rwkv_chunk_kernel(p_ref, w0_ref, a0_ref, kk_ref, ka_ref, rk_ref, wl_ref, g2_ref, ones_ref,
                       rbar_ref, mmat_ref, obar_ref, gbar_ref, gate_ref, bonus_ref,
                       *staged, d_c, tile):
    s = pl.program_id(0)
    n = len(RWKV_STAGED)
    set_a = dict(zip(RWKV_STAGED, staged[:n]))
    set_b = dict(zip(RWKV_STAGED, staged[n:]))

    @pl.when(s == 0)
    def _():
        for ref in staged[n:]:
            ref[...] = jnp.zeros_like(ref)

    def step(dst, src):
        prep = _rwkv_prep(p_ref, w0_ref, a0_ref, kk_ref, ka_ref, rk_ref, wl_ref, g2_ref, ones_ref, dst,
                          d_c=d_c, tile=tile)
        chains = _rwkv_chains(src, rbar_ref, mmat_ref, obar_ref, gbar_ref, gate_ref, bonus_ref,
                              d_c=d_c, tile=tile)
        for who in RWKV_TRACE_ORDER:
            next(prep if who == "p" else chains, None)
        for _ in chains:
            pass
        for _ in prep:
            pass

    parity = lax.rem(s, 2)

    @pl.when(parity == 0)
    def _():
        step(set_a, set_b)

    @pl.when(parity == 1)
    def _():
        step(set_b, set_a)


def _rwkv_chunk(p_c, w0, a0, k_k, k_a, r_k, w_lora, g2, ones2, tile, layer):
    bsz, seq, pc = p_c.shape
    d_c = w0.shape[-1]
    n_t = seq // tile
    n_tiles = bsz * n_t

    def staged_tile(s):
        q = jnp.minimum(s, n_tiles - 1)
        return (q // n_t, q % n_t, 0)

    def finished_tile(s):
        q = jnp.maximum(s - 1, 0)
        return (q // n_t, q % n_t, 0)

    tok = pl.BlockSpec((1, tile, d_c), finished_tile)
    shp = lambda dt: jax.ShapeDtypeStruct((bsz, seq, d_c), dt)
    one_set = ([pltpu.VMEM((tile, d_c), BF16)] * RWKV_STAGED_BF16
               + [pltpu.VMEM((tile, d_c), F32)] * (len(RWKV_STAGED) - RWKV_STAGED_BF16))
    return pl.pallas_call(
        functools.partial(_with_layer_rows(_rwkv_chunk_kernel, layer, (1, 2, 3, 4, 5)), d_c=d_c, tile=tile),
        grid=(n_tiles + 1,),
        in_specs=[pl.BlockSpec((1, tile, pc), staged_tile)]
        + [_layer_slab(a, layer) for a in (w0, a0, k_k, k_a, r_k, w_lora, g2)]
        + [pl.BlockSpec(ones2.shape, lambda s: (0, 0))],
        out_specs=[tok, tok, tok, tok, tok, tok],
        out_shape=[shp(RWKV_CARRY_DTYPE)] * 6,
        scratch_shapes=one_set + one_set,
        compiler_params=_params(("arbitrary",)),
        name="rwkv_chunk",
    )(p_c, w0, a0, k_k, k_a, r_k, w_lora, g2, ones2)


def _rwkv_scan_kernel(rbar_ref, mmat_ref, obar_ref, gbar_ref, gate_ref, bonus_ref, lnw_ref, lnb_ref, ones_ref,
                      y_ref, h_s, o_s, *, d_c, tile):
    t = pl.program_id(0)
    bsz = obar_ref.shape[0]
    n_chunks = tile // RWKV_CHUNK
    n_pairs = d_c // LANES

    @pl.when(t == 0)
    def _():
        h_s[...] = jnp.zeros_like(h_s)

    lane_lo = lax.broadcasted_iota(jnp.int32, (RWKV_CHUNK, LANES), 1) < HEAD_DIM
    chains = [(b, pi) for b in range(bsz) for pi in range(n_pairs)]

    ones2 = ones_ref[...]
    inv = 1.0 / HEAD_DIM

    def finish(c):
        rows = slice(c * RWKV_CHUNK, (c + 1) * RWKV_CHUNK)
        o = o_s[:, rows, :].reshape(bsz * RWKV_CHUNK, d_c)
        mean = _head_sums(o, ones2) * inv
        dlt = o - mean
        var = _head_sums(dlt * dlt, ones2) * inv
        gn = (dlt * lax.rsqrt(var + GN_EPS) * lnw_ref[...] + lnb_ref[...]).reshape(bsz, RWKV_CHUNK, d_c)
        y_ref[:, rows, :] = ((gn + bonus_ref[:, rows, :]) * gate_ref[:, rows, :]).astype(y_ref.dtype)

    for c in range(n_chunks):
        rows = slice(c * RWKV_CHUNK, (c + 1) * RWKV_CHUNK)
        outs = []
        for n, (b, pi) in enumerate(chains):
            lanes = slice(pi * LANES, (pi + 1) * LANES)
            hbd = _stack_pair(h_s[n], lane_lo).astype(BF16)
            lhs = jnp.concatenate([rbar_ref[b, rows, lanes], mmat_ref[b, rows, lanes]], axis=0)
            outs.append(_dot(lhs, hbd))
        if c > 0:
            finish(c - 1)
        for n, (b, pi) in enumerate(chains):
            lanes = slice(pi * LANES, (pi + 1) * LANES)
            o_s[b, rows, lanes] = outs[n][0:RWKV_CHUNK] + obar_ref[b, rows, lanes]
            h_s[n] = outs[n][RWKV_CHUNK:2 * RWKV_CHUNK] + gbar_ref[b, rows, lanes]
    finish(n_chunks - 1)


def _rwkv_scan(rbar, mmat, obar, gbar, gate, bonus, ln_w, ln_b, ones2, tile, layer):
    bsz, seq, d_c = obar.shape
    tok = pl.BlockSpec((bsz, tile, d_c), lambda t: (0, t, 0))
    return pl.pallas_call(
        functools.partial(_with_layer_rows(_rwkv_scan_kernel, layer, (6, 7)), d_c=d_c, tile=tile),
        grid=(seq // tile,),
        in_specs=[tok, tok, tok, tok, tok, tok, _layer_slab(ln_w, layer), _layer_slab(ln_b, layer),
                  pl.BlockSpec(ones2.shape, lambda t: (0, 0))],
        out_specs=tok,
        out_shape=jax.ShapeDtypeStruct((bsz, seq, d_c), BF16),
        scratch_shapes=[
            pltpu.VMEM((bsz * (d_c // LANES), RWKV_CHUNK, LANES), F32),
            pltpu.VMEM((bsz, tile, d_c), F32),
        ],
        compiler_params=_params(("arbitrary",)),
        name="rwkv_scan",
    )(rbar, mmat, obar, gbar, gate, bonus, ln_w, ln_b, ones2)


def _mixer_out(x_ref, ya_ref, yb_ref, yc_ref, mod_ref, wo_ref, g_ref, dst, *, d):
    y = _dot(jnp.concatenate([ya_ref[0], yb_ref[0], yc_ref[0]], axis=1), wo_ref[...])
    yield
    x1 = x_ref[0] + mod_ref[0, :, 2 * d:3 * d] * y
    dst["x1"][...] = x1
    n = x1 * lax.rsqrt(jnp.mean(x1 * x1, axis=-1, keepdims=True) + EPS)
    h2 = (n * g_ref[...]) * (1.0 + mod_ref[0, :, 4 * d:5 * d]) + mod_ref[0, :, 3 * d:4 * d]
    dst["h2"][...] = h2.astype(BF16)


def _conv_glu_ffn(src, mod_ref, wup_ref, cw_ref, cb_ref, wd_ref, gf_ref, o_ref, gbuf, gtail,
                  *, d, tile, d_ff, splits, final_norm):
    h2 = src["h2"][...]
    mids = []
    for lo, hi in splits:
        gbuf[0:SUBLANES, lo:hi] = gtail[:, lo:hi]
        gbuf[SUBLANES:SUBLANES + tile, lo:hi] = _dot(h2, wup_ref[:, lo:hi])
        yield
        val = _dot(h2, wup_ref[:, d_ff + lo:d_ff + hi])
        gc = cb_ref[:, lo:hi] + cw_ref[0:1, lo:hi] * gbuf[pl.ds(SUBLANES - (CONV_FF - 1), tile), lo:hi]
        for q in range(1, CONV_FF):
            gc = gc + cw_ref[q:q + 1, lo:hi] * gbuf[pl.ds(SUBLANES - (CONV_FF - 1) + q, tile), lo:hi]
        gtail[:, lo:hi] = gbuf[tile:tile + SUBLANES, lo:hi]
        mids.append((gc * jax.nn.sigmoid(gc) * val).astype(BF16))
        yield
    ffn = _dot(mids[0], wd_ref[splits[0][0]:splits[0][1], :])
    for mid, (lo, hi) in zip(mids[1:], splits[1:]):
        ffn = ffn + _dot(mid, wd_ref[lo:hi, :])
    x2 = src["x1"][...] + mod_ref[0, :, 5 * d:6 * d] * ffn
    if final_norm:
        x2 = x2 * lax.rsqrt(jnp.mean(x2 * x2, axis=-1, keepdims=True) + EPS) * gf_ref[...]
    o_ref[0] = x2


def _out_ffn_kernel(x_ref, ya_ref, yb_ref, yc_ref, mod_in_ref, mod_out_ref, wo_ref, g_ref,
                    wup_ref, cw_ref, cb_ref, wd_ref, gf_ref, o_ref,
                    gbuf, gtail, x1_a, h2_a, x1_b, h2_b, *, d, tile, d_ff, splits, tiles_per_seq, final_norm):
    s = pl.program_id(0)
    set_a = {"x1": x1_a, "h2": h2_a}
    set_b = {"x1": x1_b, "h2": h2_b}

    @pl.when(s == 0)
    def _():
        x1_b[...] = jnp.zeros_like(x1_b)
        h2_b[...] = jnp.zeros_like(h2_b)

    @pl.when(lax.rem(jnp.maximum(s - 1, 0), tiles_per_seq) == 0)
    def _():
        gtail[...] = jnp.zeros_like(gtail)

    def step(dst, src):
        head = _mixer_out(x_ref, ya_ref, yb_ref, yc_ref, mod_in_ref, wo_ref, g_ref, dst, d=d)
        ffn = _conv_glu_ffn(src, mod_out_ref, wup_ref, cw_ref, cb_ref, wd_ref, gf_ref, o_ref, gbuf, gtail,
                            d=d, tile=tile, d_ff=d_ff, splits=splits, final_norm=final_norm)
        for who in OUT_FFN_TRACE_ORDER:
            next(head if who == "h" else ffn, None)
        for _ in ffn:
            pass
        for _ in head:
            pass

    parity = lax.rem(s, 2)

    @pl.when(parity == 0)
    def _():
        step(set_a, set_b)

    @pl.when(parity == 1)
    def _():
        step(set_b, set_a)


def _column_splits(width):
    half = (width // (2 * MXU_TILE)) * MXU_TILE or width // 2
    return ((0, width - half), (width - half, width))


def _out_ffn(x, y_a, y_b, y_c, mod, w_out, g_ffn, w_up, conv_w, conv_b, w_down, g_final, tile, final_norm,
             layer):
    bsz, seq, d = x.shape
    d_ff = w_down.shape[1]
    n_t = seq // tile
    n_tiles = bsz * n_t
    resident = lambda a: _layer_slab(a, layer, single_buffer=True)

    def staged(s):
        q = jnp.minimum(s, n_tiles - 1)
        return q // n_t, q % n_t

    def finished(s):
        q = jnp.maximum(s - 1, 0)
        return q // n_t, q % n_t

    tok_in = lambda w: pl.BlockSpec((1, tile, w), lambda s: (*staged(s), 0))
    return pl.pallas_call(
        functools.partial(_with_layer_rows(_out_ffn_kernel, layer, (7, 10)), d=d, tile=tile, d_ff=d_ff,
                          splits=_column_splits(d_ff),
                          tiles_per_seq=n_t, final_norm=final_norm),
        grid=(n_tiles + 1,),
        in_specs=[
            tok_in(d), tok_in(y_a.shape[-1]), tok_in(y_b.shape[-1]), tok_in(y_c.shape[-1]),
            _mod_rows(mod, layer, lambda s: staged(s)[0]), _mod_rows(mod, layer, lambda s: finished(s)[0]),
            resident(w_out),
            _layer_slab(g_ffn, layer),
            resident(w_up),
            _layer_slab(conv_w, layer),
            _layer_slab(conv_b, layer),
            resident(w_down),
            pl.BlockSpec((1, d), lambda s: (0, 0)),
        ],
        out_specs=pl.BlockSpec((1, tile, d), lambda s: (*finished(s), 0)),
        out_shape=jax.ShapeDtypeStruct((bsz, seq, d), F32),
        scratch_shapes=[
            pltpu.VMEM((tile + SUBLANES, d_ff), F32),
            pltpu.VMEM((SUBLANES, d_ff), F32),
            pltpu.VMEM((tile, d), F32), pltpu.VMEM((tile, d), BF16),
            pltpu.VMEM((tile, d), F32), pltpu.VMEM((tile, d), BF16),
        ],
        compiler_params=_params(("arbitrary",)),
        name="out_ffn",
    )(x, y_a, y_b, y_c, mod, mod, w_out, g_ffn, w_up, conv_w, conv_b, w_down, g_final)


def _block_diag(w):
    h, n, _ = w.shape
    eye = jnp.eye(h, dtype=w.dtype)
    return (eye[:, None, :, None] * w[:, :, None, :]).reshape(h * n, h * n)


def _pick_tile(seq, want):
    tile = min(seq, want)
    while seq % tile:
        tile //= 2
    return tile


def kernel(x, c, w_mod, b_mod, norm_mix, w_in, w_out, sgu_ln_g, sgu_ln_b, sgu_w, sgu_b, lru_conv_w, lru_conv_b, lru_w_a, lru_b_a, lru_w_x, lru_b_x, lru_lambda, rwkv_mu, rwkv_w0, rwkv_w2, rwkv_a0, rwkv_a2, rwkv_g2, rwkv_k_k, rwkv_k_a, rwkv_r_k, rwkv_ln_w, rwkv_ln_b, norm_ffn, ffn_w_up, ffn_conv_w, ffn_conv_b, ffn_w_down, norm_final):
    bsz, seq, d = x.shape
    depth = w_in.shape[0]
    d_a = sgu_ln_g.shape[-1]
    d_b = lru_conv_b.shape[-1]
    d_c = rwkv_w0.shape[-1]
    p_a, p_b = 2 * d_a, 2 * d_b

    t_proj = _pick_tile(seq, 1024)
    t_sgu = _pick_tile(seq, 2048)
    t_lru = _pick_tile(seq, 512)
    t_chunk = _pick_tile(seq, 256)
    t_scan = _pick_tile(seq, 256)
    t_ffn = _pick_tile(seq, 512)

    mod = _modulation(c, w_mod, b_mod).reshape(depth, bsz, 1, N_MOD * d)
    ones_pair = _block_diag(jnp.ones((LANES // HEAD_DIM, HEAD_DIM, HEAD_DIM), BF16))
    ones2 = jnp.concatenate([ones_pair, ones_pair], axis=0)
    tril = jnp.tril(jnp.ones((SGU_CHUNK, SGU_CHUNK), dtype=bool))
    w_in_b = w_in.astype(BF16)
    w_out_b = w_out.astype(BF16)
    w_up_b = ffn_w_up.astype(BF16)
    w_down_b = ffn_w_down.astype(BF16)
    heads_per_group = LANES // HEAD_DIM
    sgu_wm = jnp.where(tril, sgu_w, 0.0).astype(BF16)
    sgu_wm = sgu_wm.reshape(depth, -1, heads_per_group, SGU_CHUNK, SGU_CHUNK)
    sgu_wm = jnp.swapaxes(sgu_wm, 2, 3).reshape(depth, -1, SGU_CHUNK, heads_per_group * SGU_CHUNK)
    sgu_bias = jnp.repeat(jnp.swapaxes(sgu_b, 1, 2), HEAD_DIM, axis=2)
    block_diag_l = jax.vmap(_block_diag)
    lru_w_bd = jnp.concatenate([block_diag_l(lru_w_a), block_diag_l(lru_w_x)], axis=2).astype(BF16)
    zeros_l = jnp.zeros((depth, LORA_W, d_c), F32)
    w_lora = jnp.concatenate([
        jnp.concatenate([rwkv_w2, zeros_l], axis=2),
        jnp.concatenate([zeros_l, rwkv_a2], axis=2)], axis=1).astype(BF16)
    g2_b = rwkv_g2.astype(BF16)

    r_k = rwkv_r_k.reshape(depth, d_c)

    for l in range(depth):
        pa, pb, pc = _in_proj(x, mod, norm_mix, rwkv_mu, w_in_b, (p_a, p_b, w_in.shape[-1] - p_a - p_b),
                              t_proj, l)
        y_a = _sgu(pa, sgu_ln_g, sgu_ln_b, sgu_wm, sgu_bias, t_sgu, l)
        y_b = _rglru(pb, lru_conv_w, lru_conv_b, lru_w_bd, lru_b_a, lru_b_x, lru_lambda, t_lru, l)
        rbar, mmat, obar, gbar, gate, bonus = _rwkv_chunk(
            pc, rwkv_w0, rwkv_a0, rwkv_k_k, rwkv_k_a, r_k, w_lora, g2_b, ones2, t_chunk, l)
        y_c = _rwkv_scan(rbar, mmat, obar, gbar, gate, bonus, rwkv_ln_w, rwkv_ln_b, ones2, t_scan, l)
        x = _out_ffn(x, y_a, y_b, y_c, mod, w_out_b, norm_ffn, w_up_b, ffn_conv_w, ffn_conv_b,
                     w_down_b, norm_final.reshape(1, -1), t_ffn, l == depth - 1, l)
    return x
```

```python
import functools

import jax
import jax.numpy as jnp
from jax import lax
from jax.experimental import pallas as pl
from jax.experimental.pallas import tpu as pltpu

HEAD_DIM = 64
SGU_CHUNK = 128
CONV_B = 4
LRU_C = 8.0
LORA_W = 64
LORA_A = 64
LORA_G = 128
CONV_FF = 3
N_MOD = 6
EPS = 1e-6
LN_EPS = 1e-5
GN_EPS = 64e-5

RWKV_CHUNK = 64
LANES = 128
MXU_TILE = 256
IN_PROJ_ROW_BLOCK = 256
OUT_FFN_TRACE_ORDER = "hfhfff"
SUBLANES = 8
VMEM_LIMIT = 56 * 1024 * 1024
F32_MIN_NORMAL = 1.1754944e-38

BF16 = jnp.bfloat16
F32 = jnp.float32


def _dot(a, b):
    return jnp.dot(a, b, preferred_element_type=F32)


def _dot_nt(a, b):
    return lax.dot_general(a, b, (((1,), (1,)), ((), ())), preferred_element_type=F32)


def _dot_tn(a, b):
    return lax.dot_general(a, b, (((0,), (0,)), ((), ())), preferred_element_type=F32)


def _split(x):
    hi = x.astype(BF16)
    lo = (x - hi.astype(F32)).astype(BF16)
    return hi, lo


def _head_sums(x, ones2, split=True):
    outs = []
    for pi in range(x.shape[-1] // LANES):
        xp = x[:, pi * LANES:(pi + 1) * LANES]
        if split:
            hi, lo = _split(xp)
            outs.append(_dot(jnp.concatenate([hi, lo], axis=1), ones2))
        else:
            outs.append(_dot(xp.astype(BF16), ones2[0:LANES]))
    return jnp.concatenate(outs, axis=1)


def _sigmoid(z):
    return 0.5 * jnp.tanh(0.5 * z) + 0.5


def _softplus(z, small_values_matter=True):
    t = jnp.exp(-jnp.abs(z))
    return jnp.maximum(z, 0.0) + (jnp.log1p(t) if small_values_matter else jnp.log(1.0 + t))


def _params(sem):
    return pltpu.CompilerParams(dimension_semantics=sem, vmem_limit_bytes=VMEM_LIMIT)


def _layer_slab(arr, layer, single_buffer=False):
    if arr.ndim == 2:
        return pl.BlockSpec(arr.shape, lambda *_: (0, 0))
    zeros = (0,) * (arr.ndim - 1)
    mode = {"pipeline_mode": pl.Buffered(1)} if single_buffer else {}
    return pl.BlockSpec((None,) + arr.shape[1:], lambda *_: (layer,) + zeros, **mode)


def _with_layer_rows(kernel_fn, layer, positions):
    def narrowed(*refs, **kwargs):
        refs = list(refs)
        for i in positions:
            refs[i] = refs[i].at[pl.ds(layer, 1)]
        return kernel_fn(*refs, **kwargs)
    return narrowed


def _mod_rows(mod, layer, batch_of):
    return pl.BlockSpec((None, 1, 1, mod.shape[-1]), lambda *g: (layer, batch_of(*g), 0, 0))


def _mod_kernel(c_ref, w_ref, b_ref, o_ref):
    c = c_ref[...]
    ca = c * jax.nn.sigmoid(c)
    w = w_ref[0]
    chi, clo = _split(ca)
    whi, wlo = _split(w)
    acc = _dot(chi, whi) + _dot(chi, wlo) + _dot(clo, whi)
    o_ref[0] = acc + b_ref[0]


def _modulation(c, w_mod, b_mod):
    depth, d, nd = w_mod.shape
    bsz = c.shape[0]
    nblk = nd // d
    return pl.pallas_call(
        _mod_kernel,
        grid=(depth, nblk),
        in_specs=[
            pl.BlockSpec((bsz, d), lambda l, j: (0, 0)),
            pl.BlockSpec((1, d, d), lambda l, j: (l, 0, j)),
            pl.BlockSpec((1, 1, d), lambda l, j: (l, 0, j)),
        ],
        out_specs=pl.BlockSpec((1, bsz, d), lambda l, j: (l, 0, j)),
        out_shape=jax.ShapeDtypeStruct((depth, bsz, nd), F32),
        compiler_params=_params(("arbitrary", "arbitrary")),
        name="adaln_mod",
    )(c, w_mod, b_mod.reshape(depth, 1, nd))


def _in_proj_kernel(x_ref, mod_ref, g_ref, mu_ref, w_ref, pa_ref, pb_ref, pc_ref, last_row,
                    *, d, tile, row_block):
    na, nb, nc = pa_ref.shape[-1], pb_ref.shape[-1], pc_ref.shape[-1]

    @pl.when(pl.program_id(1) == 0)
    def _():
        last_row[...] = jnp.zeros_like(last_row)

    scale = 1.0 + mod_ref[0, :, d:2 * d]
    shift = mod_ref[0, :, 0:d]
    first_row = lax.broadcasted_iota(jnp.int32, (row_block, nc), 0) == 0
    carry = last_row[...]
    for r in range(tile // row_block):
        rows = slice(r * row_block, (r + 1) * row_block)
        x = x_ref[0, rows, :]
        y = x * lax.rsqrt(jnp.mean(x * x, axis=-1, keepdims=True) + EPS)
        h = (y * g_ref[...]) * scale + shift
        p = _dot(h.astype(BF16), w_ref[...])
        pa_ref[0, rows, :] = p[:, 0:na]
        pb_ref[0, rows, :] = p[:, na:na + nb]
        pc = p[:, na + nb:]
        prev = jnp.where(first_row, carry, pltpu.roll(pc, 1, 0))
        pc_ref[0, rows, :] = pc + (prev - pc) * mu_ref[...]
        carry = pc[row_block - 1:row_block, :]
    last_row[...] = carry


def _in_proj(x, mod, g, mu, w_in, widths, tile, layer):
    bsz, seq, d = x.shape
    na, nb, nc = widths
    return pl.pallas_call(
        functools.partial(_with_layer_rows(_in_proj_kernel, layer, (2, 3)), d=d, tile=tile,
                          row_block=min(tile, IN_PROJ_ROW_BLOCK)),
        grid=(bsz, seq // tile),
        in_specs=[
            pl.BlockSpec((1, tile, d), lambda b, t: (b, t, 0)),
            _mod_rows(mod, layer, lambda b, t: b),
            _layer_slab(g, layer),
            _layer_slab(mu, layer),
            _layer_slab(w_in, layer, single_buffer=True),
        ],
        out_specs=[
            pl.BlockSpec((1, tile, na), lambda b, t: (b, t, 0)),
            pl.BlockSpec((1, tile, nb), lambda b, t: (b, t, 0)),
            pl.BlockSpec((1, tile, nc), lambda b, t: (b, t, 0)),
        ],
        out_shape=[
            jax.ShapeDtypeStruct((bsz, seq, na), F32),
            jax.ShapeDtypeStruct((bsz, seq, nb), F32),
            jax.ShapeDtypeStruct((bsz, seq, nc), F32),
        ],
        scratch_shapes=[pltpu.VMEM((1, nc), F32)],
        compiler_params=_params(("arbitrary", "arbitrary")),
        name="in_proj",
    )(x, mod, g, mu, w_in)


def _sgu_kernel(p_ref, lng_ref, lnb_ref, w_ref, bias_ref, o_ref, *, d_a, n_chunks):
    z = jax.nn.gelu(p_ref[0])
    u = z[:, :d_a]
    v = z[:, d_a:]
    mu = jnp.mean(v, axis=-1, keepdims=True)
    var = jnp.mean(jnp.square(v - mu), axis=-1, keepdims=True)
    vn = ((v - mu) * lax.rsqrt(var + LN_EPS) * lng_ref[...] + lnb_ref[...]).astype(BF16)
    lane_lo = lax.broadcasted_iota(jnp.int32, (SGU_CHUNK, LANES), 1) < HEAD_DIM
    rows = []
    for c in range(n_chunks):
        cols = []
        for q in range(d_a // LANES):
            vq = vn[c * SGU_CHUNK:(c + 1) * SGU_CHUNK, q * LANES:(q + 1) * LANES]
            cols.append(_dot(w_ref[q], _stack_pair(vq, lane_lo)))
        rows.append(jnp.concatenate(cols, axis=1) + bias_ref[...])
    mixed = jnp.concatenate(rows, axis=0) if n_chunks > 1 else rows[0]
    o_ref[0] = (u * mixed).astype(o_ref.dtype)


def _sgu(p_a, ln_g, ln_b, w_masked, bias_cols, tile, layer):
    bsz, seq, two_da = p_a.shape
    d_a = two_da // 2
    return pl.pallas_call(
        functools.partial(_with_layer_rows(_sgu_kernel, layer, (1, 2)), d_a=d_a, n_chunks=tile // SGU_CHUNK),
        grid=(bsz, seq // tile),
        in_specs=[
            pl.BlockSpec((1, tile, two_da), lambda b, t: (b, t, 0)),
            _layer_slab(ln_g, layer), _layer_slab(ln_b, layer),
            _layer_slab(w_masked, layer), _layer_slab(bias_cols, layer),
        ],
        out_specs=pl.BlockSpec((1, tile, d_a), lambda b, t: (b, t, 0)),
        out_shape=jax.ShapeDtypeStruct((bsz, seq, d_a), BF16),
        compiler_params=_params(("arbitrary", "arbitrary")),
        name="sgu",
    )(p_a, ln_g, ln_b, w_masked, bias_cols)


def _rglru_kernel(p_ref, cw_ref, cb_ref, wbd_ref, bra_ref, bix_ref, lam_ref, o_ref,
                  xtail, hprev, *, d_b, tile):
    t = pl.program_id(1)

    @pl.when(t == 0)
    def _():
        xtail[...] = jnp.zeros_like(xtail)
        hprev[...] = jnp.zeros_like(hprev)

    n_grp = tile // SUBLANES
    sub = lax.broadcasted_iota(jnp.int32, (n_grp, SUBLANES, d_b), 1)

    x_new = p_ref[0, :, 0:d_b]
    yg = p_ref[0, :, d_b:2 * d_b]
    x_ext = jnp.concatenate([xtail[...], x_new], axis=0).reshape(n_grp + 1, SUBLANES, d_b)
    xtail[...] = x_new[tile - SUBLANES:tile, :]
    xr3 = cb_ref[...] + cw_ref[CONV_B - 1:CONV_B, :] * x_ext[1:]
    for k in range(1, CONV_B):
        rot = pltpu.roll(x_ext, k, 1)
        xr3 = xr3 + cw_ref[CONV_B - 1 - k:CONV_B - k, :] * jnp.where(sub >= k, rot[1:], rot[:-1])
    xr = xr3.reshape(tile, d_b)

    ri = _dot(xr.astype(BF16), wbd_ref[...])
    r = _sigmoid(ri[:, 0:d_b] + bra_ref[...])
    i = _sigmoid(ri[:, d_b:2 * d_b] + bix_ref[...])
    log_a = (-LRU_C * r) * _softplus(-lam_ref[...])
    a = jnp.exp(log_a)
    one_minus_a2 = -jnp.tanh(log_a) * (a * a + 1.0)
    bterm = (one_minus_a2 * lax.rsqrt(jnp.maximum(one_minus_a2, F32_MIN_NORMAL))) * (i * xr)

    a3 = a.reshape(n_grp, SUBLANES, d_b)
    b3 = bterm.reshape(n_grp, SUBLANES, d_b)
    shift = 1
    while shift < SUBLANES:
        keep = sub >= shift
        a_s = jnp.where(keep, pltpu.roll(a3, shift, 1), 1.0)
        b_s = jnp.where(keep, pltpu.roll(b3, shift, 1), 0.0)
        b3 = a3 * b_s + b3
        a3 = a3 * a_s
        shift *= 2
    h = hprev[...]
    groups = []
    for g in range(n_grp):
        hg = b3[g] + a3[g] * h
        groups.append(hg)
        h = hg[SUBLANES - 1:SUBLANES, :]
    hprev[...] = h
    o_ref[0] = (jax.nn.gelu(yg) * jnp.concatenate(groups, axis=0)).astype(o_ref.dtype)


def _rglru(p_b, conv_w, conv_b, w_bd, b_ra, b_ix, lam, tile, layer):
    bsz, seq, two_db = p_b.shape
    d_b = two_db // 2
    return pl.pallas_call(
        functools.partial(_with_layer_rows(_rglru_kernel, layer, (2, 4, 5, 6)), d_b=d_b, tile=tile),
        grid=(bsz, seq // tile),
        in_specs=[pl.BlockSpec((1, tile, two_db), lambda b, t: (b, t, 0))]
        + [_layer_slab(a, layer) for a in (conv_w, conv_b, w_bd, b_ra, b_ix, lam)],
        out_specs=pl.BlockSpec((1, tile, d_b), lambda b, t: (b, t, 0)),
        out_shape=jax.ShapeDtypeStruct((bsz, seq, d_b), BF16),
        scratch_shapes=[
            pltpu.VMEM((SUBLANES, d_b), F32),
            pltpu.VMEM((1, d_b), F32),
        ],
        compiler_params=_params(("arbitrary", "arbitrary")),
        name="rglru",
    )(p_b, conv_w, conv_b, w_bd, b_ra, b_ix, lam)


def _stack_pair(x, lane_lo):
    return jnp.concatenate([jnp.where(lane_lo, x, 0.0), jnp.where(lane_lo, 0.0, x)], axis=0)


RWKV_STAGED = ("kt", "kh", "bh", "khl", "bhl", "v", "rtb", "rt", "gl", "gate", "bonus")
RWKV_STAGED_BF16 = 7
RWKV_TRACE_ORDER = "ccpcccpcccpccp"
RWKV_CARRY_DTYPE = BF16


def _rwkv_prep(p_ref, w0_ref, a0_ref, kk_ref, ka_ref, rk_ref, wl_ref, g2_ref, ones_ref, dst, *, d_c, tile):
    n_chunks = tile // RWKV_CHUNK
    ps = p_ref[0]

    r = ps[:, 0:d_c]
    k = ps[:, d_c:2 * d_c]
    v = ps[:, 2 * d_c:3 * d_c]
    xwa = ps[:, 3 * d_c:3 * d_c + LORA_W + LORA_A]
    xg = ps[:, 3 * d_c + LORA_W + LORA_A:]

    lane_l = lax.broadcasted_iota(jnp.int32, xwa.shape, 1)
    lin = jnp.where(lane_l < LORA_W, jnp.tanh(xwa), xwa).astype(BF16)
    wa_lora = _dot(lin, wl_ref[...])
    w = -_softplus(-(w0_ref[...] + wa_lora[:, 0:d_c]), small_values_matter=False) - 0.5
    lw = -jnp.exp(w)
    a = jax.nn.sigmoid(a0_ref[...] + wa_lora[:, d_c:2 * d_c])
    dst["gate"][...] = _dot(jax.nn.sigmoid(xg).astype(BF16), g2_ref[...])

    yield
    ones2 = ones_ref[...]
    kk = k * kk_ref[...]
    kappa = kk * lax.rsqrt(jnp.maximum(_head_sums(kk * kk, ones2, split=False), 1e-24))
    kmod = k * (1.0 + (a - 1.0) * ka_ref[...])
    bvec = kappa * a
    yield
    dst["bonus"][...] = _head_sums(r * kmod * rk_ref[...], ones2, split=False) * v

    ri = lax.broadcasted_iota(jnp.int32, (LANES, 2 * LANES), 0)
    ci = lax.broadcasted_iota(jnp.int32, (LANES, 2 * LANES), 1) & (LANES - 1)
    shift = RWKV_CHUNK.bit_length() - 1
    tril2 = jnp.where((jnp.right_shift(ri, shift) == jnp.right_shift(ci, shift)) & (ci <= ri),
                      1.0, 0.0).astype(BF16)
    lw_hi, lw_lo = _split(lw)
    c_parts = []
    for m in range(tile // LANES):
        rws = slice(m * LANES, (m + 1) * LANES)
        c_parts.append(_dot(tril2, jnp.concatenate([lw_hi[rws], lw_lo[rws]], axis=0)))
    c_incl = jnp.concatenate(c_parts, axis=0) if len(c_parts) > 1 else c_parts[0]
    c_tot = jnp.concatenate(
        [jnp.broadcast_to(c_incl[(c + 1) * RWKV_CHUNK - 1:(c + 1) * RWKV_CHUNK, :], (RWKV_CHUNK, d_c))
         for c in range(n_chunks)], axis=0)

    yield
    e_neg = jnp.exp(-c_incl)
    e_end = jnp.exp(c_tot - c_incl)
    dst["kt"][...] = (kappa * jnp.exp(c_incl - lw)).astype(BF16)
    rt = r * jnp.exp(c_incl)
    dst["rt"][...] = rt
    dst["rtb"][...] = rt.astype(BF16)
    dst["kh"][...] = (kmod * e_neg).astype(BF16)
    dst["bh"][...] = (bvec * e_neg).astype(BF16)
    dst["khl"][...] = (kmod * e_end).astype(BF16)
    dst["bhl"][...] = (bvec * e_end).astype(BF16)
    dst["v"][...] = v.astype(BF16)
    dst["gl"][...] = jnp.exp(c_tot)


def _rwkv_chains(src, rbar_ref, mmat_ref, obar_ref, gbar_ref, gate_ref, bonus_ref, *, d_c, tile):
    n_chunks = tile // RWKV_CHUNK
    n_pairs = d_c // LANES
    gate_ref[0] = src["gate"][...].astype(gate_ref.dtype)
    bonus_ref[0] = src["bonus"][...].astype(bonus_ref.dtype)

    ii = lax.broadcasted_iota(jnp.int32, (LANES, LANES), 0)
    jj = lax.broadcasted_iota(jnp.int32, (LANES, LANES), 1)
    shift = RWKV_CHUNK.bit_length() - 1
    same_head = jnp.right_shift(ii, shift) == jnp.right_shift(jj, shift)
    strict = same_head & (ii > jj)
    incl = same_head & (ii >= jj)
    eye = ii == jj
    lane_lo = lax.broadcasted_iota(jnp.int32, (RWKV_CHUNK, LANES), 1) < HEAD_DIM
    pair_diag = (lax.broadcasted_iota(jnp.int32, (RWKV_CHUNK, LANES), 1) & (HEAD_DIM - 1)) == \
        lax.broadcasted_iota(jnp.int32, (RWKV_CHUNK, LANES), 0)

    chains = [(c, pi) for c in range(n_chunks) for pi in range(n_pairs)]

    def rows_of(ch):
        return slice(ch[0] * RWKV_CHUNK, (ch[0] + 1) * RWKV_CHUNK)

    def lanes_of(ch):
        return slice(ch[1] * LANES, (ch[1] + 1) * LANES)

    def stacked(name, ch):
        return _stack_pair(src[name][rows_of(ch), lanes_of(ch)], lane_lo)

    def doubled(name, ch):
        x = src[name][rows_of(ch), lanes_of(ch)]
        return jnp.concatenate([x, x], axis=0)

    def per_head(x):
        return jnp.where(lane_lo, x[0:RWKV_CHUNK], x[RWKV_CHUNK:2 * RWKV_CHUNK])

    kts_b = [doubled("kt", ch) for ch in chains]
    vbd = [doubled("v", ch) for ch in chains]

    sc = [_dot_nt(jnp.concatenate([kts_b[n], doubled("rtb", ch)], axis=0),
                  jnp.concatenate([stacked("kh", ch), stacked("bh", ch)], axis=0))
          for n, ch in enumerate(chains)]
    a_k = [jnp.where(strict, s[0:LANES, 0:LANES], 0.0).astype(BF16) for s in sc]
    n_b = [jnp.where(strict, s[0:LANES, LANES:2 * LANES], 0.0) for s in sc]
    p_m = [jnp.where(incl, s[LANES:2 * LANES, 0:LANES], 0.0).astype(BF16) for s in sc]
    q_m = [jnp.where(incl, s[LANES:2 * LANES, LANES:2 * LANES], 0.0).astype(BF16) for s in sc]
    yield

    x_m = [jnp.where(eye, 1.0, 0.0) - n for n in n_b]
    pw = [_dot(n.astype(BF16), n.astype(BF16)) for n in n_b]
    yield
    for _ in range(4):
        y = [_dot(jnp.concatenate([x.astype(BF16), p.astype(BF16)], axis=0), p.astype(BF16))
             for x, p in zip(x_m, pw)]
        x_m = [x + yy[0:LANES] for x, yy in zip(x_m, y)]
        pw = [yy[LANES:2 * LANES] for yy in y]
        yield
    t_m = [(x + _dot(x.astype(BF16), p.astype(BF16))).astype(BF16) for x, p in zip(x_m, pw)]
    yield

    akv = [_dot(a, vv) for a, vv in zip(a_k, vbd)]
    yield
    ty_b = [_dot(tm, jnp.concatenate([kb, a.astype(BF16)], axis=1)).astype(BF16)
            for tm, kb, a in zip(t_m, kts_b, akv)]
    yield
    rhs = [jnp.concatenate([jnp.concatenate([jnp.zeros_like(vv), vv], axis=1), tyb], axis=0)
           for vv, tyb in zip(vbd, ty_b)]
    tok = [_dot(jnp.concatenate([p, -q], axis=1), rr) for p, q, rr in zip(p_m, q_m, rhs)]
    chn = [_dot_tn(jnp.concatenate([stacked("khl", ch), -stacked("bhl", ch)], axis=0), rr)
           for ch, rr in zip(chains, rhs)]
    yield

    for n, ch in enumerate(chains):
        rows, lanes = rows_of(ch), lanes_of(ch)
        rbar = src["rt"][rows, lanes] + per_head(tok[n][:, 0:LANES])
        gl_row = src["gl"][ch[0] * RWKV_CHUNK:ch[0] * RWKV_CHUNK + 1, lanes]
        mmat = jnp.where(pair_diag, gl_row, 0.0) + per_head(chn[n][:, 0:LANES])
        rbar_ref[0, rows, lanes] = rbar.astype(rbar_ref.dtype)
        mmat_ref[0, rows, lanes] = mmat.astype(mmat_ref.dtype)
        obar_ref[0, rows, lanes] = per_head(tok[n][:, LANES:2 * LANES]).astype(obar_ref.dtype)
        gbar_ref[0, rows, lanes] = per_head(chn[n][:, LANES:2 * LANES]).astype(gbar_ref.dtype)


def _rwkv_chunk_kernel(p_ref, w0_ref, a0_ref, kk_ref, ka_ref, rk_ref, wl_ref, g2_ref, ones_ref,
                       rbar_ref, mmat_ref, obar_ref, gbar_ref, gate_ref, bonus_ref,
                       *staged, d_c, tile):
    s = pl.program_id(0)
    n = len(RWKV_STAGED)
    set_a = dict(zip(RWKV_STAGED, staged[:n]))
    set_b = dict(zip(RWKV_STAGED, staged[n:]))

    @pl.when(s == 0)
    def _():
        for ref in staged[n:]:
            ref[...] = jnp.zeros_like(ref)

    def step(dst, src):
        prep = _rwkv_prep(p_ref, w0_ref, a0_ref, kk_ref, ka_ref, rk_ref, wl_ref, g2_ref, ones_ref, dst,
                          d_c=d_c, tile=tile)
        chains = _rwkv_chains(src, rbar_ref, mmat_ref, obar_ref, gbar_ref, gate_ref, bonus_ref,
                              d_c=d_c, tile=tile)
        for who in RWKV_TRACE_ORDER:
            next(prep if who == "p" else chains, None)
        for _ in chains:
            pass
        for _ in prep:
            pass

    parity = lax.rem(s, 2)

    @pl.when(parity == 0)
    def _():
        step(set_a, set_b)

    @pl.when(parity == 1)
    def _():
        step(set_b, set_a)


def _rwkv_chunk(p_c, w0, a0, k_k, k_a, r_k, w_lora, g2, ones2, tile, layer):
    bsz, seq, pc = p_c.shape
    d_c = w0.shape[-1]
    n_t = seq // tile
    n_tiles = bsz * n_t

    def staged_tile(s):
        q = jnp.minimum(s, n_tiles - 1)
        return (q // n_t, q % n_t, 0)

    def finished_tile(s):
        q = jnp.maximum(s - 1, 0)
        return (q // n_t, q % n_t, 0)

    tok = pl.BlockSpec((1, tile, d_c), finished_tile)
    shp = lambda dt: jax.ShapeDtypeStruct((bsz, seq, d_c), dt)
    one_set = ([pltpu.VMEM((tile, d_c), BF16)] * RWKV_STAGED_BF16
               + [pltpu.VMEM((tile, d_c), F32)] * (len(RWKV_STAGED) - RWKV_STAGED_BF16))
    return pl.pallas_call(
        functools.partial(_with_layer_rows(_rwkv_chunk_kernel, layer, (1, 2, 3, 4, 5)), d_c=d_c, tile=tile),
        grid=(n_tiles + 1,),
        in_specs=[pl.BlockSpec((1, tile, pc), staged_tile)]
        + [_layer_slab(a, layer) for a in (w0, a0, k_k, k_a, r_k, w_lora, g2)]
        + [pl.BlockSpec(ones2.shape, lambda s: (0, 0))],
        out_specs=[tok, tok, tok, tok, tok, tok],
        out_shape=[shp(RWKV_CARRY_DTYPE)] * 6,
        scratch_shapes=one_set + one_set,
        compiler_params=_params(("arbitrary",)),
        name="rwkv_chunk",
    )(p_c, w0, a0, k_k, k_a, r_k, w_lora, g2, ones2)


def _rwkv_scan_kernel(rbar_ref, mmat_ref, obar_ref, gbar_ref, gate_ref, bonus_ref, lnw_ref, lnb_ref, ones_ref,
                      y_ref, h_s, o_s, *, d_c, tile):
    t = pl.program_id(0)
    bsz = obar_ref.shape[0]
    n_chunks = tile // RWKV_CHUNK
    n_pairs = d_c // LANES

    @pl.when(t == 0)
    def _():
        h_s[...] = jnp.zeros_like(h_s)

    lane_lo = lax.broadcasted_iota(jnp.int32, (RWKV_CHUNK, LANES), 1) < HEAD_DIM
    chains = [(b, pi) for b in range(bsz) for pi in range(n_pairs)]

    ones2 = ones_ref[...]
    inv = 1.0 / HEAD_DIM

    def finish(c):
        rows = slice(c * RWKV_CHUNK, (c + 1) * RWKV_CHUNK)
        o = o_s[:, rows, :].reshape(bsz * RWKV_CHUNK, d_c)
        mean = _head_sums(o, ones2) * inv
        dlt = o - mean
        var = _head_sums(dlt * dlt, ones2) * inv
        gn = (dlt * lax.rsqrt(var + GN_EPS) * lnw_ref[...] + lnb_ref[...]).reshape(bsz, RWKV_CHUNK, d_c)
        y_ref[:, rows, :] = ((gn + bonus_ref[:, rows, :]) * gate_ref[:, rows, :]).astype(y_ref.dtype)

    for c in range(n_chunks):
        rows = slice(c * RWKV_CHUNK, (c + 1) * RWKV_CHUNK)
        outs = []
        for n, (b, pi) in enumerate(chains):
            lanes = slice(pi * LANES, (pi + 1) * LANES)
            hbd = _stack_pair(h_s[n], lane_lo).astype(BF16)
            lhs = jnp.concatenate([rbar_ref[b, rows, lanes], mmat_ref[b, rows, lanes]], axis=0)
            outs.append(_dot(lhs, hbd))
        if c > 0:
            finish(c - 1)
        for n, (b, pi) in enumerate(chains):
            lanes = slice(pi * LANES, (pi + 1) * LANES)
            o_s[b, rows, lanes] = outs[n][0:RWKV_CHUNK] + obar_ref[b, rows, lanes]
            h_s[n] = outs[n][RWKV_CHUNK:2 * RWKV_CHUNK] + gbar_ref[b, rows, lanes]
    finish(n_chunks - 1)


def _rwkv_scan(rbar, mmat, obar, gbar, gate, bonus, ln_w, ln_b, ones2, tile, layer):
    bsz, seq, d_c = obar.shape
    tok = pl.BlockSpec((bsz, tile, d_c), lambda t: (0, t, 0))
    return pl.pallas_call(
        functools.partial(_with_layer_rows(_rwkv_scan_kernel, layer, (6, 7)), d_c=d_c, tile=tile),
        grid=(seq // tile,),
        in_specs=[tok, tok, tok, tok, tok, tok, _layer_slab(ln_w, layer), _layer_slab(ln_b, layer),
                  pl.BlockSpec(ones2.shape, lambda t: (0, 0))],
        out_specs=tok,
        out_shape=jax.ShapeDtypeStruct((bsz, seq, d_c), BF16),
        scratch_shapes=[
            pltpu.VMEM((bsz * (d_c // LANES), RWKV_CHUNK, LANES), F32),
            pltpu.VMEM((bsz, tile, d_c), F32),
        ],
        compiler_params=_params(("arbitrary",)),
        name="rwkv_scan",
    )(rbar, mmat, obar, gbar, gate, bonus, ln_w, ln_b, ones2)


def _mixer_out(x_ref, ya_ref, yb_ref, yc_ref, mod_ref, wo_ref, g_ref, dst, *, d):
    y = _dot(jnp.concatenate([ya_ref[0], yb_ref[0], yc_ref[0]], axis=1), wo_ref[...])
    yield
    x1 = x_ref[0] + mod_ref[0, :, 2 * d:3 * d] * y
    dst["x1"][...] = x1
    n = x1 * lax.rsqrt(jnp.mean(x1 * x1, axis=-1, keepdims=True) + EPS)
    h2 = (n * g_ref[...]) * (1.0 + mod_ref[0, :, 4 * d:5 * d]) + mod_ref[0, :, 3 * d:4 * d]
    dst["h2"][...] = h2.astype(BF16)


def _conv_glu_ffn(src, mod_ref, wup_ref, cw_ref, cb_ref, wd_ref, gf_ref, o_ref, gbuf, gtail,
                  *, d, tile, d_ff, splits, final_norm):
    h2 = src["h2"][...]
    mids = []
    for lo, hi in splits:
        gbuf[0:SUBLANES, lo:hi] = gtail[:, lo:hi]
        gbuf[SUBLANES:SUBLANES + tile, lo:hi] = _dot(h2, wup_ref[:, lo:hi])
        yield
        val = _dot(h2, wup_ref[:, d_ff + lo:d_ff + hi])
        gc = cb_ref[:, lo:hi] + cw_ref[0:1, lo:hi] * gbuf[pl.ds(SUBLANES - (CONV_FF - 1), tile), lo:hi]
        for q in range(1, CONV_FF):
            gc = gc + cw_ref[q:q + 1, lo:hi] * gbuf[pl.ds(SUBLANES - (CONV_FF - 1) + q, tile), lo:hi]
        gtail[:, lo:hi] = gbuf[tile:tile + SUBLANES, lo:hi]
        mids.append((gc * jax.nn.sigmoid(gc) * val).astype(BF16))
        yield
    ffn = _dot(mids[0], wd_ref[splits[0][0]:splits[0][1], :])
    for mid, (lo, hi) in zip(mids[1:], splits[1:]):
        ffn = ffn + _dot(mid, wd_ref[lo:hi, :])
    x2 = src["x1"][...] + mod_ref[0, :, 5 * d:6 * d] * ffn
    if final_norm:
        x2 = x2 * lax.rsqrt(jnp.mean(x2 * x2, axis=-1, keepdims=True) + EPS) * gf_ref[...]
    o_ref[0] = x2


def _out_ffn_kernel(x_ref, ya_ref, yb_ref, yc_ref, mod_in_ref, mod_out_ref, wo_ref, g_ref,
                    wup_ref, cw_ref, cb_ref, wd_ref, gf_ref, o_ref,
                    gbuf, gtail, x1_a, h2_a, x1_b, h2_b, *, d, tile, d_ff, splits, tiles_per_seq, final_norm):
    s = pl.program_id(0)
    set_a = {"x1": x1_a, "h2": h2_a}
    set_b = {"x1": x1_b, "h2": h2_b}

    @pl.when(s == 0)
    def _():
        x1_b[...] = jnp.zeros_like(x1_b)
        h2_b[...] = jnp.zeros_like(h2_b)

    @pl.when(lax.rem(jnp.maximum(s - 1, 0), tiles_per_seq) == 0)
    def _():
        gtail[...] = jnp.zeros_like(gtail)

    def step(dst, src):
        head = _mixer_out(x_ref, ya_ref, yb_ref, yc_ref, mod_in_ref, wo_ref, g_ref, dst, d=d)
        ffn = _conv_glu_ffn(src, mod_out_ref, wup_ref, cw_ref, cb_ref, wd_ref, gf_ref, o_ref, gbuf, gtail,
                            d=d, tile=tile, d_ff=d_ff, splits=splits, final_norm=final_norm)
        for who in OUT_FFN_TRACE_ORDER:
            next(head if who == "h" else ffn, None)
        for _ in ffn:
            pass
        for _ in head:
            pass

    parity = lax.rem(s, 2)

    @pl.when(parity == 0)
    def _():
        step(set_a, set_b)

    @pl.when(parity == 1)
    def _():
        step(set_b, set_a)


def _column_splits(width):
    half = (width // (2 * MXU_TILE)) * MXU_TILE or width // 2
    return ((0, width - half), (width - half, width))


def _out_ffn(x, y_a, y_b, y_c, mod, w_out, g_ffn, w_up, conv_w, conv_b, w_down, g_final, tile, final_norm,
             layer):
    bsz, seq, d = x.shape
    d_ff = w_down.shape[1]
    n_t = seq // tile
    n_tiles = bsz * n_t
    resident = lambda a: _layer_slab(a, layer, single_buffer=True)

    def staged(s):
        q = jnp.minimum(s, n_tiles - 1)
        return q // n_t, q % n_t

    def finished(s):
        q = jnp.maximum(s - 1, 0)
        return q // n_t, q % n_t

    tok_in = lambda w: pl.BlockSpec((1, tile, w), lambda s: (*staged(s), 0))
    return pl.pallas_call(
        functools.partial(_with_layer_rows(_out_ffn_kernel, layer, (7, 10)), d=d, tile=tile, d_ff=d_ff,
                          splits=_column_splits(d_ff),
                          tiles_per_seq=n_t, final_norm=final_norm),
        grid=(n_tiles + 1,),
        in_specs=[
            tok_in(d), tok_in(y_a.shape[-1]), tok_in(y_b.shape[-1]), tok_in(y_c.shape[-1]),
            _mod_rows(mod, layer, lambda s: staged(s)[0]), _mod_rows(mod, layer, lambda s: finished(s)[0]),
            resident(w_out),
            _layer_slab(g_ffn, layer),
            resident(w_up),
            _layer_slab(conv_w, layer),
            _layer_slab(conv_b, layer),
            resident(w_down),
            pl.BlockSpec((1, d), lambda s: (0, 0)),
        ],
        out_specs=pl.BlockSpec((1, tile, d), lambda s: (*finished(s), 0)),
        out_shape=jax.ShapeDtypeStruct((bsz, seq, d), F32),
        scratch_shapes=[
            pltpu.VMEM((tile + SUBLANES, d_ff), F32),
            pltpu.VMEM((SUBLANES, d_ff), F32),
            pltpu.VMEM((tile, d), F32), pltpu.VMEM((tile, d), BF16),
            pltpu.VMEM((tile, d), F32), pltpu.VMEM((tile, d), BF16),
        ],
        compiler_params=_params(("arbitrary",)),
        name="out_ffn",
    )(x, y_a, y_b, y_c, mod, mod, w_out, g_ffn, w_up, conv_w, conv_b, w_down, g_final)


def _block_diag(w):
    h, n, _ = w.shape
    eye = jnp.eye(h, dtype=w.dtype)
    return (eye[:, None, :, None] * w[:, :, None, :]).reshape(h * n, h * n)


def _pick_tile(seq, want):
    tile = min(seq, want)
    while seq % tile:
        tile //= 2
    return tile


def kernel(x, c, w_mod, b_mod, norm_mix, w_in, w_out, sgu_ln_g, sgu_ln_b, sgu_w, sgu_b, lru_conv_w, lru_conv_b, lru_w_a, lru_b_a, lru_w_x, lru_b_x, lru_lambda, rwkv_mu, rwkv_w0, rwkv_w2, rwkv_a0, rwkv_a2, rwkv_g2, rwkv_k_k, rwkv_k_a, rwkv_r_k, rwkv_ln_w, rwkv_ln_b, norm_ffn, ffn_w_up, ffn_conv_w, ffn_conv_b, ffn_w_down, norm_final):
    bsz, seq, d = x.shape
    depth = w_in.shape[0]
    d_a = sgu_ln_g.shape[-1]
    d_b = lru_conv_b.shape[-1]
    d_c = rwkv_w0.shape[-1]
    p_a, p_b = 2 * d_a, 2 * d_b

    t_proj = _pick_tile(seq, 1024)
    t_sgu = _pick_tile(seq, 2048)
    t_lru = _pick_tile(seq, 512)
    t_chunk = _pick_tile(seq, 512)
    t_scan = _pick_tile(seq, 256)
    t_ffn = _pick_tile(seq, 512)

    mod = _modulation(c, w_mod, b_mod).reshape(depth, bsz, 1, N_MOD * d)
    ones_pair = _block_diag(jnp.ones((LANES // HEAD_DIM, HEAD_DIM, HEAD_DIM), BF16))
    ones2 = jnp.concatenate([ones_pair, ones_pair], axis=0)
    tril = jnp.tril(jnp.ones((SGU_CHUNK, SGU_CHUNK), dtype=bool))
    w_in_b = w_in.astype(BF16)
    w_out_b = w_out.astype(BF16)
    w_up_b = ffn_w_up.astype(BF16)
    w_down_b = ffn_w_down.astype(BF16)
    heads_per_group = LANES // HEAD_DIM
    sgu_wm = jnp.where(tril, sgu_w, 0.0).astype(BF16)
    sgu_wm = sgu_wm.reshape(depth, -1, heads_per_group, SGU_CHUNK, SGU_CHUNK)
    sgu_wm = jnp.swapaxes(sgu_wm, 2, 3).reshape(depth, -1, SGU_CHUNK, heads_per_group * SGU_CHUNK)
    sgu_bias = jnp.repeat(jnp.swapaxes(sgu_b, 1, 2), HEAD_DIM, axis=2)
    block_diag_l = jax.vmap(_block_diag)
    lru_w_bd = jnp.concatenate([block_diag_l(lru_w_a), block_diag_l(lru_w_x)], axis=2).astype(BF16)
    zeros_l = jnp.zeros((depth, LORA_W, d_c), F32)
    w_lora = jnp.concatenate([
        jnp.concatenate([rwkv_w2, zeros_l], axis=2),
        jnp.concatenate([zeros_l, rwkv_a2], axis=2)], axis=1).astype(BF16)
    g2_b = rwkv_g2.astype(BF16)

    r_k = rwkv_r_k.reshape(depth, d_c)

    for l in range(depth):
        pa, pb, pc = _in_proj(x, mod, norm_mix, rwkv_mu, w_in_b, (p_a, p_b, w_in.shape[-1] - p_a - p_b),
                              t_proj, l)
        y_a = _sgu(pa, sgu_ln_g, sgu_ln_b, sgu_wm, sgu_bias, t_sgu, l)
        y_b = _rglru(pb, lru_conv_w, lru_conv_b, lru_w_bd, lru_b_a, lru_b_x, lru_lambda, t_lru, l)
        rbar, mmat, obar, gbar, gate, bonus = _rwkv_chunk(
            pc, rwkv_w0, rwkv_a0, rwkv_k_k, rwkv_k_a, r_k, w_lora, g2_b, ones2, t_chunk, l)
        y_c = _rwkv_scan(rbar, mmat, obar, gbar, gate, bonus, rwkv_ln_w, rwkv_ln_b, ones2, t_scan, l)
        x = _out_ffn(x, y_a, y_b, y_c, mod, w_out_b, norm_ffn, w_up_b, ffn_conv_w, ffn_conv_b,
                     w_down_b, norm_final.reshape(1, -1), t_ffn, l == depth - 1, l)
    return x
```
